```python
import jax, jax.numpy as jnp
from jax import lax
import numpy as np

D_MODEL = 4096
BATCH = 4
SEQ = 4096
DEPTH = 4

CHUNK = 64
CONV_CH = D_MODEL // 2
CONV_WIDTH = 31
GLA_HEADS = 4
GLA_DK = D_MODEL // 4 // GLA_HEADS
GLA_DV = D_MODEL // 2 // GLA_HEADS
GLA_LOWRANK = 16
GLA_TAU = 16.0
W_MIX = CONV_CH + GLA_HEADS * GLA_DV
D_FF = 4 * D_MODEL
N_MOD = 6
NORM_EPS = 1e-6
IN_COLS = 2 * CONV_CH + 2 * GLA_HEADS * GLA_DK + 2 * GLA_HEADS * GLA_DV + GLA_LOWRANK

kernel_name = 'hybrid_conformer_gla_adaln_trunk'


def _rmsnorm(x, g):
    xf = x.astype(jnp.float32)
    y = xf * lax.rsqrt(jnp.mean(xf * xf, axis=-1, keepdims=True) + NORM_EPS)
    return (y * g.astype(jnp.float32)).astype(x.dtype)


def _modulate(h, shift, scale):
    return h * (1 + scale[:, None, :]) + shift[:, None, :]


def _conformer_conv(u_val, u_gate, conv_w, conv_b, ln_g, ln_b):
    u = u_val * jax.nn.sigmoid(u_gate)
    u = jnp.pad(u, ((0, 0), (CONV_WIDTH - 1, 0), (0, 0)))
    y = lax.conv_general_dilated(u, conv_w[:, None, :].astype(u.dtype), (1,), 'VALID',
                                 dimension_numbers=('NWC', 'WIO', 'NWC'),
                                 feature_group_count=CONV_CH) + conv_b
    yf = y.astype(jnp.float32)
    mu = jnp.mean(yf, axis=-1, keepdims=True)
    var = jnp.mean(jnp.square(yf - mu), axis=-1, keepdims=True)
    yn = (yf - mu) * lax.rsqrt(var + NORM_EPS) * ln_g.astype(jnp.float32) + ln_b.astype(jnp.float32)
    return jax.nn.silu(yn).astype(u_val.dtype)


def _gla(q, k, v, g, a_lr, w_alpha, b_alpha, norm_g):
    f32 = jnp.float32
    b, s, _ = q.shape
    n = s // CHUNK

    def heads(t, d):
        return t.astype(f32).reshape(b, n, CHUNK, GLA_HEADS, d).transpose(0, 3, 1, 2, 4)

    log_a = jax.nn.log_sigmoid((a_lr @ w_alpha + b_alpha).astype(f32)) / GLA_TAU
    cum = jnp.cumsum(heads(log_a, GLA_DK), axis=3)
    qh = heads(q, GLA_DK) * GLA_DK ** -0.5
    kh = heads(k, GLA_DK)
    vh = heads(v, GLA_DV)
    q_dec = qh * jnp.exp(cum)
    k_in = kh * jnp.exp(-cum)
    k_st = kh * jnp.exp(cum[..., -1:, :] - cum)
    chunk_decay = jnp.exp(cum[..., -1, :])

    causal = jnp.tril(jnp.ones((CHUNK, CHUNK), dtype=bool))
    scores = jnp.einsum('bhnld,bhnmd->bhnlm', q_dec, k_in)
    scores = jnp.where(causal, scores, 0.0)
    o_intra = jnp.einsum('bhnlm,bhnmv->bhnlv', scores, vh)

    def step(state, inp):
        qd, ks, vc, dec = inp
        o = jnp.einsum('bhld,bhdv->bhlv', qd, state)
        state = dec[..., None] * state + jnp.einsum('bhld,bhlv->bhdv', ks, vc)
        return state, o

    state0 = jnp.zeros((b, GLA_HEADS, GLA_DK, GLA_DV), f32)
    xs = (jnp.moveaxis(q_dec, 2, 0), jnp.moveaxis(k_st, 2, 0),
          jnp.moveaxis(vh, 2, 0), jnp.moveaxis(chunk_decay, 2, 0))
    _, o_inter = lax.scan(step, state0, xs)
    o = o_intra + jnp.moveaxis(o_inter, 0, 2)
    o = o.transpose(0, 2, 3, 1, 4).reshape(b, s, GLA_HEADS, GLA_DV)
    o = o * lax.rsqrt(jnp.mean(o * o, axis=-1, keepdims=True) + NORM_EPS) * norm_g.astype(f32)
    o = o.reshape(b, s, GLA_HEADS * GLA_DV) * jax.nn.silu(g.astype(f32))
    return o.astype(q.dtype)


def setup_inputs(seed: int = 0) -> dict:
    key = jax.random.key(seed)
    ks = jax.random.split(key, 20)
    f32 = jnp.float32

    def nrm(k, shape, scale):
        return jax.random.normal(k, shape, f32) * scale

    x = nrm(ks[0], (BATCH, SEQ, D_MODEL), 1.0)
    c = nrm(ks[1], (BATCH, D_MODEL), 1.0)
    w_ada = nrm(ks[2], (D_MODEL, N_MOD * D_MODEL), 0.1 * D_MODEL ** -0.5)
    b_ada = nrm(ks[3], (N_MOD * D_MODEL,), 0.02)
    mod_table = nrm(ks[4], (DEPTH, N_MOD, D_MODEL), 0.1)
    mod_table = mod_table.at[:, 2::3, :].add(0.5)
    norm1_g = 1.0 + nrm(ks[5], (DEPTH, D_MODEL), 0.05)
    w_in = nrm(ks[6], (DEPTH, D_MODEL, IN_COLS), D_MODEL ** -0.5)
    conv_w = nrm(ks[7], (DEPTH, CONV_WIDTH, CONV_CH), CONV_WIDTH ** -0.5)
    conv_b = nrm(ks[8], (DEPTH, CONV_CH), 0.02)
    conv_ln_g = 1.0 + nrm(ks[9], (DEPTH, CONV_CH), 0.05)
    conv_ln_b = nrm(ks[10], (DEPTH, CONV_CH), 0.02)
    w_alpha = nrm(ks[11], (DEPTH, GLA_LOWRANK, GLA_HEADS * GLA_DK), GLA_LOWRANK ** -0.5)
    b_alpha = nrm(ks[12], (DEPTH, GLA_HEADS * GLA_DK), 0.1)
    gla_norm_g = 1.0 + nrm(ks[13], (DEPTH, GLA_DV), 0.05)
    w_out = nrm(ks[14], (DEPTH, W_MIX, D_MODEL), W_MIX ** -0.5)
    norm2_g = 1.0 + nrm(ks[15], (DEPTH, D_MODEL), 0.05)
    w_mlp1 = nrm(ks[16], (DEPTH, D_MODEL, D_FF), D_MODEL ** -0.5)
    w_mlp2 = nrm(ks[17], (DEPTH, D_FF, D_MODEL), D_FF ** -0.5)
    final_g = 1.0 + nrm(ks[18], (D_MODEL,), 0.05)
    return {'x': x, 'c': c, 'w_ada': w_ada, 'b_ada': b_ada, 'mod_table': mod_table,
            'norm1_g': norm1_g, 'w_in': w_in, 'conv_w': conv_w, 'conv_b': conv_b,
            'conv_ln_g': conv_ln_g, 'conv_ln_b': conv_ln_b, 'w_alpha': w_alpha,
            'b_alpha': b_alpha, 'gla_norm_g': gla_norm_g, 'w_out': w_out,
            'norm2_g': norm2_g, 'w_mlp1': w_mlp1, 'w_mlp2': w_mlp2, 'final_g': final_g}


def reference(x, c, w_ada, b_ada, mod_table, norm1_g, w_in, conv_w, conv_b, conv_ln_g,
              conv_ln_b, w_alpha, b_alpha, gla_norm_g, w_out, norm2_g, w_mlp1, w_mlp2, final_g):
    f32 = jnp.float32
    mod = (jax.nn.silu(c.astype(f32)) @ w_ada.astype(f32) + b_ada.astype(f32))
    mod = mod.reshape(c.shape[0], N_MOD, D_MODEL)
    sizes = (CONV_CH, CONV_CH, GLA_HEADS * GLA_DK, GLA_HEADS * GLA_DK,
             GLA_HEADS * GLA_DV, GLA_HEADS * GLA_DV)
    offsets = np.cumsum(sizes).tolist()
    for l in range(DEPTH):
        m = (mod + mod_table[l].astype(f32)).astype(x.dtype)
        h = _modulate(_rmsnorm(x, norm1_g[l]), m[:, 0], m[:, 1])
        z = h @ w_in[l]
        a_val, a_gate, q, k, v, g, a_lr = jnp.split(z, offsets, axis=-1)
        y_a = _conformer_conv(a_val, a_gate, conv_w[l], conv_b[l], conv_ln_g[l], conv_ln_b[l])
        y_b = _gla(q, k, v, g, a_lr, w_alpha[l], b_alpha[l], gla_norm_g[l])
        y = jnp.concatenate([y_a, y_b], axis=-1) @ w_out[l]
        x = x + m[:, 2, None, :] * y
        h = _modulate(_rmsnorm(x, norm2_g[l]), m[:, 3], m[:, 4])
        f = jnp.square(jax.nn.relu(h @ w_mlp1[l])) @ w_mlp2[l]
        x = x + m[:, 5, None, :] * f
    return _rmsnorm(x, final_g)
```

```python
import functools

import jax
import jax.numpy as jnp
from jax import lax
from jax.experimental import pallas as pl
from jax.experimental.pallas import tpu as pltpu

F32 = jnp.float32
BF16 = jnp.bfloat16

CHUNK = 64
CONV_WIDTH = 31
CONV_HALO = 32
GLA_HEADS = 4
GLA_LOWRANK = 16
GLA_TAU = 16.0
N_MOD = 6
NORM_EPS = 1e-6
LANES = 128
VMEM_LIMIT = 56 * 1024 * 1024


def _cparams(sem):
    return pltpu.CompilerParams(dimension_semantics=sem, vmem_limit_bytes=VMEM_LIMIT)


def _sigmoid(x):
    return 1.0 / (1.0 + jnp.exp(-x))


def _silu(x):
    return x * _sigmoid(x)


def _ada_kernel(c_ref, w_ref, b_ref, t_ref, o_ref):
    s = _silu(c_ref[...]).astype(BF16)
    r = jnp.dot(s, w_ref[...].astype(BF16), preferred_element_type=F32) + b_ref[...]
    o_ref[...] = r[None, :, :] + t_ref[...]


def _ada(c8, w_ada, b_ada, table, tn=1024):
    d, n = w_ada.shape
    depth = table.shape[0]
    return pl.pallas_call(
        _ada_kernel,
        grid=(n // tn,),
        in_specs=[
            pl.BlockSpec((8, d), lambda j: (0, 0)),
            pl.BlockSpec((d, tn), lambda j: (0, j)),
            pl.BlockSpec((1, tn), lambda j: (0, j)),
            pl.BlockSpec((depth, 1, tn), lambda j: (0, 0, j)),
        ],
        out_specs=pl.BlockSpec((depth, 8, tn), lambda j: (0, 0, j)),
        out_shape=jax.ShapeDtypeStruct((depth, 8, n), F32),
        compiler_params=_cparams(("arbitrary",)),
        name="ada_mod",
    )(c8, w_ada, b_ada, table)


def _norm_modulate(x_ref, g_ref, shift, scale, h_ref, rows_per_iter=32):
    tm = x_ref.shape[0]
    g = g_ref[...]
    one_plus = 1.0 + scale

    def body(r, carry):
        sl = pl.ds(pl.multiple_of(r * rows_per_iter, rows_per_iter), rows_per_iter)
        xf = x_ref[sl, :]
        ms = jnp.mean(xf * xf, axis=-1, keepdims=True)
        y = xf * lax.rsqrt(ms + NORM_EPS) * g
        h_ref[sl, :] = (y * one_plus + shift).astype(BF16)
        return carry

    lax.fori_loop(0, tm // rows_per_iter, body, 0)


def _inproj_kernel(x_ref, g_ref, sh_ref, sc_ref, w_ref, wlr_ref, z_ref, alr_ref, h_ref, *, blocks_per_batch):
    @pl.when(pl.program_id(1) == 0)
    def _():
        b = pl.program_id(0) // blocks_per_batch
        _norm_modulate(x_ref, g_ref, sh_ref[pl.ds(b, 1), :], sc_ref[pl.ds(b, 1), :], h_ref)
        alr_ref[...] = jnp.dot(h_ref[...], wlr_ref[...], preferred_element_type=F32)

    z_ref[...] = jnp.dot(h_ref[...], w_ref[...], preferred_element_type=F32).astype(z_ref.dtype)


def _inproj(x, norm_g, mods, w_main, w_lr, layer, seq, tm=512, tn=1024):
    t, d = x.shape
    n = w_main.shape[-1]
    return pl.pallas_call(
        functools.partial(_inproj_kernel, blocks_per_batch=seq // tm),
        grid=(t // tm, n // tn),
        in_specs=[
            pl.BlockSpec((tm, d), lambda i, j: (i, 0)),
            pl.BlockSpec((None, 1, d), lambda i, j: (layer, 0, 0)),
            pl.BlockSpec((None, 8, d), lambda i, j: (layer, 0, 0)),
            pl.BlockSpec((None, 8, d), lambda i, j: (layer, 0, 1)),
            pl.BlockSpec((None, d, tn), lambda i, j: (layer, 0, j)),
            pl.BlockSpec((None, d, LANES), lambda i, j: (layer, 0, 0)),
        ],
        out_specs=[
            pl.BlockSpec((tm, tn), lambda i, j: (i, j)),
            pl.BlockSpec((tm, LANES), lambda i, j: (i, 0)),
        ],
        out_shape=[
            jax.ShapeDtypeStruct((t, n), F32),
            jax.ShapeDtypeStruct((t, LANES), F32),
        ],
        scratch_shapes=[pltpu.VMEM((tm, d), BF16)],
        compiler_params=_cparams(("parallel", "arbitrary")),
        name="in_proj",
    )(x, norm_g, mods, mods, w_main, w_lr)


def _conv_kernel(av_ref, ag_ref, hv_ref, hg_ref, cw_ref, cb_ref, lg_ref, lb_ref, o_ref, u_ref, y_ref,
                 *, blocks_per_batch, lane_tile):
    n = pl.program_id(0)
    l = pl.program_id(1)
    ts = av_ref.shape[0]
    first = (n % blocks_per_batch) == 0

    hu = hv_ref[...] * _sigmoid(hg_ref[...])
    u_ref[0:CONV_HALO, :] = jnp.where(first, 0.0, hu)
    u_ref[CONV_HALO:, :] = av_ref[...] * _sigmoid(ag_ref[...])

    off = CONV_HALO - (CONV_WIDTH - 1)
    for tt in range(ts // CHUNK):
        for lt in range(lane_tile // LANES):
            ls = slice(lt * LANES, (lt + 1) * LANES)
            acc = jnp.broadcast_to(cb_ref[:, ls], (CHUNK, LANES))
            for w in range(CONV_WIDTH):
                r0 = tt * CHUNK + off + w
                acc = acc + u_ref[r0:r0 + CHUNK, ls] * cw_ref[w:w + 1, ls]
            y_ref[l, tt * CHUNK:(tt + 1) * CHUNK, ls] = acc

    @pl.when(l == pl.num_programs(1) - 1)
    def _():
        nl = y_ref.shape[0]
        c = nl * lane_tile
        rows = 32

        def body(r, carry):
            sl = pl.ds(pl.multiple_of(r * rows, rows), rows)
            ys = [y_ref[k, sl, :] for k in range(nl)]
            mu = sum(jnp.sum(v, axis=-1, keepdims=True) for v in ys) / c
            var = sum(jnp.sum(jnp.square(v - mu), axis=-1, keepdims=True) for v in ys) / c
            inv = lax.rsqrt(var + NORM_EPS)
            for k in range(nl):
                cs = slice(k * lane_tile, (k + 1) * lane_tile)
                yn = (ys[k] - mu) * inv * lg_ref[:, cs] + lb_ref[:, cs]
                o_ref[sl, cs] = _silu(yn).astype(o_ref.dtype)
            return carry

        lax.fori_loop(0, ts // rows, body, 0)


def _conv(z, conv_w, conv_b, ln_g, ln_b, layer, seq, conv_ch, ts=512, lane_tile=512):
    t = z.shape[0]
    nl = conv_ch // lane_tile
    hb = ts // CONV_HALO

    def halo_idx(col0):
        return lambda n, l: (jnp.maximum(n * hb - 1, 0), col0 + l)

    return pl.pallas_call(
        functools.partial(_conv_kernel, blocks_per_batch=seq // ts, lane_tile=lane_tile),
        grid=(t // ts, nl),
        in_specs=[
            pl.BlockSpec((ts, lane_tile), lambda n, l: (n, l)),
            pl.BlockSpec((ts, lane_tile), lambda n, l: (n, nl + l)),
            pl.BlockSpec((CONV_HALO, lane_tile), halo_idx(0)),
            pl.BlockSpec((CONV_HALO, lane_tile), halo_idx(nl)),
            pl.BlockSpec((None, CONV_HALO, lane_tile), lambda n, l: (layer, 0, l)),
            pl.BlockSpec((None, 1, lane_tile), lambda n, l: (layer, 0, l)),
            pl.BlockSpec((None, 1, conv_ch), lambda n, l: (layer, 0, 0)),
            pl.BlockSpec((None, 1, conv_ch), lambda n, l: (layer, 0, 0)),
        ],
        out_specs=pl.BlockSpec((ts, conv_ch), lambda n, l: (n, 0)),
        out_shape=jax.ShapeDtypeStruct((t, conv_ch), BF16),
        scratch_shapes=[
            pltpu.VMEM((CONV_HALO + ts, lane_tile), F32),
            pltpu.VMEM((nl, ts, lane_tile), F32),
        ],
        compiler_params=_cparams(("parallel", "arbitrary")),
        name="conformer_conv",
    )(z, z, z, z, conv_w, conv_b, ln_g, ln_b)


def _gla_kernel(q_ref, k_ref, v_ref, g_ref, alr_ref, wa_ref, ba_ref, ng_ref, o_ref, st_ref, *, q_scale):
    @pl.when(pl.program_id(2) == 0)
    def _():
        st_ref[...] = jnp.zeros_like(st_ref)

    rows, dk = q_ref.shape
    x = jnp.dot(alr_ref[...].astype(BF16), wa_ref[...], preferred_element_type=F32) + ba_ref[...]
    log_a = -(jnp.maximum(-x, 0.0) + jnp.log1p(jnp.exp(-jnp.abs(x)))) * (1.0 / GLA_TAU)
    sub = lax.broadcasted_iota(jnp.int32, (rows, dk), 0) & 7
    c = log_a
    for s in (1, 2, 4):
        c = c + jnp.where(sub >= s, pltpu.roll(c, s, axis=0), 0.0)

    tril = (lax.broadcasted_iota(jnp.int32, (CHUNK, CHUNK), 0)
            >= lax.broadcasted_iota(jnp.int32, (CHUNK, CHUNK), 1))
    contract_last = (((1,), (1,)), ((), ()))
    contract_first = (((0,), (0,)), ((), ()))

    for ci in range(rows // CHUNK):
        base = ci * CHUNK
        parts, carry = [], None
        for j in range(CHUNK // 8):
            blk = c[base + 8 * j:base + 8 * j + 8, :]
            if carry is not None:
                blk = blk + carry
            parts.append(blk)
            carry = blk[7:8, :]
        cum = jnp.concatenate(parts, axis=0)
        cl = carry
        rs = slice(base, base + CHUNK)
        kc = k_ref[rs, :]
        q_dec = ((q_ref[rs, :] * q_scale) * jnp.exp(cum)).astype(BF16)
        k_in = (kc * jnp.exp(-cum)).astype(BF16)
        k_st = (kc * jnp.exp(cl - cum)).astype(BF16)
        vb = v_ref[rs, :].astype(BF16)

        scores = lax.dot_general(q_dec, k_in, contract_last, preferred_element_type=F32)
        scores = jnp.where(tril, scores, 0.0).astype(BF16)
        st = st_ref[...]
        o = (jnp.dot(scores, vb, preferred_element_type=F32)
             + lax.dot_general(q_dec, st.astype(BF16), contract_last, preferred_element_type=F32))
        upd = lax.dot_general(vb, k_st, contract_first, preferred_element_type=F32)
        st_ref[...] = st * jnp.exp(cl) + upd

        ms = jnp.mean(o * o, axis=-1, keepdims=True)
        on = o * lax.rsqrt(ms + NORM_EPS) * ng_ref[...]
        o_ref[rs, :] = (on * _silu(g_ref[rs, :])).astype(o_ref.dtype)


def _gla(z, alr, w_alpha, b_alpha, norm_g, layer, batch, seq, conv_ch, dk, dv, rows=512):
    t = z.shape[0]
    nb = seq // rows
    q_col = 2 * conv_ch // dk
    k_col = q_col + GLA_HEADS
    v_col = (2 * conv_ch + 2 * GLA_HEADS * dk) // dv
    g_col = v_col + GLA_HEADS
    return pl.pallas_call(
        functools.partial(_gla_kernel, q_scale=float(dk) ** -0.5),
        grid=(batch, GLA_HEADS, nb),
        in_specs=[
            pl.BlockSpec((rows, dk), lambda b, h, n: (b * nb + n, q_col + h)),
            pl.BlockSpec((rows, dk), lambda b, h, n: (b * nb + n, k_col + h)),
            pl.BlockSpec((rows, dv), lambda b, h, n: (b * nb + n, v_col + h)),
            pl.BlockSpec((rows, dv), lambda b, h, n: (b * nb + n, g_col + h)),
            pl.BlockSpec((rows, LANES), lambda b, h, n: (b * nb + n, 0)),
            pl.BlockSpec((None, LANES, dk), lambda b, h, n: (layer, 0, h)),
            pl.BlockSpec((None, 1, dk), lambda b, h, n: (layer, 0, h)),
            pl.BlockSpec((None, 1, dv), lambda b, h, n: (layer, 0, 0)),
        ],
        out_specs=pl.BlockSpec((rows, dv), lambda b, h, n: (b * nb + n, h)),
        out_shape=jax.ShapeDtypeStruct((t, GLA_HEADS * dv), BF16),
        scratch_shapes=[pltpu.VMEM((dv, dk), F32)],
        compiler_params=_cparams(("parallel", "parallel", "arbitrary")),
        name="gla",
    )(z, z, z, z, alr, w_alpha, b_alpha, norm_g)


def _outproj_kernel(ya_ref, yb_ref, w_ref, x_ref, gt_ref, o_ref, *, blocks_per_batch):
    ca = ya_ref.shape[1]
    y = (jnp.dot(ya_ref[...], w_ref[0:ca, :], preferred_element_type=F32)
         + jnp.dot(yb_ref[...], w_ref[ca:, :], preferred_element_type=F32))
    b = pl.program_id(0) // blocks_per_batch
    o_ref[...] = x_ref[...] + gt_ref[pl.ds(b, 1), :] * y


def _outproj(ya, yb, w_out, x, mods, layer, seq, tm=1024, tn=1024):
    t, d = x.shape
    ca, cb = ya.shape[1], yb.shape[1]
    return pl.pallas_call(
        functools.partial(_outproj_kernel, blocks_per_batch=seq // tm),
        grid=(t // tm, d // tn),
        in_specs=[
            pl.BlockSpec((tm, ca), lambda i, j: (i, 0)),
            pl.BlockSpec((tm, cb), lambda i, j: (i, 0)),
            pl.BlockSpec((None, ca + cb, tn), lambda i, j: (layer, 0, j)),
            pl.BlockSpec((tm, tn), lambda i, j: (i, j)),
            pl.BlockSpec((None, 8, tn), lambda i, j: (layer, 0, 2 * (d // tn) + j)),
        ],
        out_specs=pl.BlockSpec((tm, tn), lambda i, j: (i, j)),
        out_shape=jax.ShapeDtypeStruct((t, d), F32),
        compiler_params=_cparams(("parallel", "arbitrary")),
        name="out_proj",
    )(ya, yb, w_out, x, mods)


def _mlp1_kernel(x_ref, g_ref, sh_ref, sc_ref, w_ref, o_ref, h_ref, *, blocks_per_batch):
    @pl.when(pl.program_id(1) == 0)
    def _():
        b = pl.program_id(0) // blocks_per_batch
        _norm_modulate(x_ref, g_ref, sh_ref[pl.ds(b, 1), :], sc_ref[pl.ds(b, 1), :], h_ref)

    a = jnp.maximum(jnp.dot(h_ref[...], w_ref[...], preferred_element_type=F32), 0.0)
    o_ref[...] = (a * a).astype(o_ref.dtype)


def _mlp1(x, norm_g, mods, w1, layer, seq, tm=512, tn=1024):
    t, d = x.shape
    n = w1.shape[-1]
    return pl.pallas_call(
        functools.partial(_mlp1_kernel, blocks_per_batch=seq // tm),
        grid=(t // tm, n // tn),
        in_specs=[
            pl.BlockSpec((tm, d), lambda i, j: (i, 0)),
            pl.BlockSpec((None, 1, d), lambda i, j: (layer, 0, 0)),
            pl.BlockSpec((None, 8, d), lambda i, j: (layer, 0, 3)),
            pl.BlockSpec((None, 8, d), lambda i, j: (layer, 0, 4)),
            pl.BlockSpec((None, d, tn), lambda i, j: (layer, 0, j)),
        ],
        out_specs=pl.BlockSpec((tm, tn), lambda i, j: (i, j)),
        out_shape=jax.ShapeDtypeStruct((t, n), BF16),
        scratch_shapes=[pltpu.VMEM((tm, d), BF16)],
        compiler_params=_cparams(("parallel", "arbitrary")),
        name="mlp_up",
    )(x, norm_g, mods, mods, w1)


def _mlp2_kernel(a_ref, w_ref, x_ref, gt_ref, o_ref, acc_ref, *, blocks_per_batch):
    kk = pl.program_id(2)
    p = jnp.dot(a_ref[...], w_ref[...], preferred_element_type=F32)

    @pl.when(kk == 0)
    def _():
        acc_ref[...] = p

    @pl.when(kk > 0)
    def _():
        acc_ref[...] += p

    @pl.when(kk == pl.num_programs(2) - 1)
    def _():
        b = pl.program_id(0) // blocks_per_batch
        o_ref[...] = x_ref[...] + gt_ref[pl.ds(b, 1), :] * acc_ref[...]


def _mlp2(a, w2, x, mods, layer, seq, tm=1024, tn=1024, tk=2048):
    t, d = x.shape
    kdim = a.shape[1]
    return pl.pallas_call(
        functools.partial(_mlp2_kernel, blocks_per_batch=seq // tm),
        grid=(t // tm, d // tn, kdim // tk),
        in_specs=[
            pl.BlockSpec((tm, tk), lambda i, j, k: (i, k)),
            pl.BlockSpec((None, tk, tn), lambda i, j, k: (layer, k, j)),
            pl.BlockSpec((tm, tn), lambda i, j, k: (i, j)),
            pl.BlockSpec((None, 8, tn), lambda i, j, k: (layer, 0, 5 * (d // tn) + j)),
        ],
        out_specs=pl.BlockSpec((tm, tn), lambda i, j, k: (i, j)),
        out_shape=jax.ShapeDtypeStruct((t, d), F32),
        scratch_shapes=[pltpu.VMEM((tm, tn), F32)],
        compiler_params=_cparams(("parallel", "parallel", "arbitrary")),
        name="mlp_down",
    )(a, w2, x, mods)


def _final_norm_kernel(x_ref, g_ref, o_ref):
    rows = 32
    g = g_ref[...]

    def body(r, carry):
        sl = pl.ds(pl.multiple_of(r * rows, rows), rows)
        xf = x_ref[sl, :]
        ms = jnp.mean(xf * xf, axis=-1, keepdims=True)
        o_ref[sl, :] = xf * lax.rsqrt(ms + NORM_EPS) * g
        return carry

    lax.fori_loop(0, x_ref.shape[0] // rows, body, 0)


def _final_norm(x, g, tm=512):
    t, d = x.shape
    return pl.pallas_call(
        _final_norm_kernel,
        grid=(t // tm,),
        in_specs=[pl.BlockSpec((tm, d), lambda i: (i, 0)), pl.BlockSpec((1, d), lambda i: (0, 0))],
        out_specs=pl.BlockSpec((tm, d), lambda i: (i, 0)),
        out_shape=jax.ShapeDtypeStruct((t, d), F32),
        compiler_params=_cparams(("parallel",)),
        name="final_norm",
    )(x, g)


def kernel(x, c, w_ada, b_ada, mod_table, norm1_g, w_in, conv_w, conv_b, conv_ln_g, conv_ln_b, w_alpha,
           b_alpha, gla_norm_g, w_out, norm2_g, w_mlp1, w_mlp2, final_g):
    batch, seq, d = x.shape
    depth = mod_table.shape[0]
    conv_ch = conv_w.shape[-1]
    dk = w_alpha.shape[-1] // GLA_HEADS
    dv = gla_norm_g.shape[-1]
    n_main = w_in.shape[-1] - GLA_LOWRANK
    assert batch <= 8 and mod_table.shape[1] == N_MOD and conv_w.shape[1] == CONV_WIDTH
    assert n_main == 2 * conv_ch + 2 * GLA_HEADS * (dk + dv) and w_alpha.shape[1] == GLA_LOWRANK

    w_main = w_in[:, :, :n_main].astype(BF16)
    w_lr = jnp.pad(w_in[:, :, n_main:], ((0, 0), (0, 0), (0, LANES - GLA_LOWRANK))).astype(BF16)
    wa = jnp.pad(w_alpha, ((0, 0), (0, LANES - GLA_LOWRANK), (0, 0))).astype(BF16)
    w_out_b = w_out.astype(BF16)
    w1 = w_mlp1.astype(BF16)
    w2 = w_mlp2.astype(BF16)
    cw = jnp.pad(conv_w, ((0, 0), (0, CONV_HALO - CONV_WIDTH), (0, 0)))
    c8 = jnp.pad(c.astype(F32), ((0, 8 - batch), (0, 0)))

    mods = _ada(c8, w_ada, b_ada.reshape(1, -1), mod_table.reshape(depth, 1, N_MOD * d))

    xf = x.reshape(batch * seq, d)
    for l in range(depth):
        z, alr = _inproj(xf, norm1_g.reshape(depth, 1, d), mods, w_main, w_lr, l, seq)
        y_a = _conv(z, cw, conv_b.reshape(depth, 1, conv_ch), conv_ln_g.reshape(depth, 1, conv_ch),
                    conv_ln_b.reshape(depth, 1, conv_ch), l, seq, conv_ch)
        y_b = _gla(z, alr, wa, b_alpha.reshape(depth, 1, -1), gla_norm_g.reshape(depth, 1, dv),
                   l, batch, seq, conv_ch, dk, dv)
        xf = _outproj(y_a, y_b, w_out_b, xf, mods, l, seq)
        hid = _mlp1(xf, norm2_g.reshape(depth, 1, d), mods, w1, l, seq)
        xf = _mlp2(hid, w2, xf, mods, l, seq)
    out = _final_norm(xf, final_g.reshape(1, d))
    return out.reshape(batch, seq, d)
```

```python
import functools

import jax
import jax.numpy as jnp
from jax import lax
from jax.experimental import pallas as pl
from jax.experimental.pallas import tpu as pltpu

F32 = jnp.float32
BF16 = jnp.bfloat16

CHUNK = 64
CONV_WIDTH = 31
CONV_HALO = 32
GLA_HEADS = 4
GLA_LOWRANK = 16
GLA_TAU = 16.0
N_MOD = 6
NORM_EPS = 1e-6
LANES = 128
SUBLANES = 8
NORM_ROWS = 16
VMEM_LIMIT = 56 * 1024 * 1024


def _cparams(sem):
    return pltpu.CompilerParams(dimension_semantics=sem, vmem_limit_bytes=VMEM_LIMIT)


def _sigmoid(x):
    return 1.0 / (1.0 + jnp.exp(-x))


def _silu(x):
    return x * _sigmoid(x)


def _ada_kernel(c_ref, w_ref, b_ref, t_ref, o_ref):
    s = _silu(c_ref[...]).astype(BF16)
    r = jnp.dot(s, w_ref[...].astype(BF16), preferred_element_type=F32) + b_ref[...]
    o_ref[...] = r[None, :, :] + t_ref[...]


def _ada(c8, w_ada, b_ada, table, tn=1024):
    d, n = w_ada.shape
    depth = table.shape[0]
    return pl.pallas_call(
        _ada_kernel,
        grid=(n // tn,),
        in_specs=[
            pl.BlockSpec((8, d), lambda j: (0, 0)),
            pl.BlockSpec((d, tn), lambda j: (0, j)),
            pl.BlockSpec((1, tn), lambda j: (0, j)),
            pl.BlockSpec((depth, 1, tn), lambda j: (0, 0, j)),
        ],
        out_specs=pl.BlockSpec((depth, 8, tn), lambda j: (0, 0, j)),
        out_shape=jax.ShapeDtypeStruct((depth, 8, n), F32),
        compiler_params=_cparams(("arbitrary",)),
        name="ada_mod",
    )(c8, w_ada, b_ada, table)


def _norm_modulate(x_ref, g_ref, shift, scale, h_ref, p_ref):
    tm, d = x_ref.shape
    sub = p_ref.shape[1]
    groups = NORM_ROWS // sub
    n = tm // NORM_ROWS
    p_ref[0] = jnp.broadcast_to(g_ref[...], (sub, d))
    p_ref[1] = jnp.broadcast_to(1.0 + scale, (sub, d))
    p_ref[2] = jnp.broadcast_to(shift, (sub, d))

    def rows_of(r):
        return pl.ds(pl.multiple_of(r * NORM_ROWS, NORM_ROWS), NORM_ROWS)

    def sum_sq(r):
        xf = x_ref[rows_of(r), :].reshape(groups, sub, d)
        return jnp.sum(xf * xf, axis=-1, keepdims=True)

    def body(r, ss):
        ss_next = sum_sq(jnp.minimum(r + 1, n - 1))
        xf = x_ref[rows_of(r), :].reshape(groups, sub, d)
        y = xf * lax.rsqrt(ss * (1.0 / d) + NORM_EPS) * p_ref[0][None]
        h = y * p_ref[1][None] + p_ref[2][None]
        h_ref[rows_of(r), :] = h.reshape(NORM_ROWS, d).astype(BF16)
        return ss_next

    lax.fori_loop(0, n, body, sum_sq(0))


def _inproj_kernel(x_ref, g_ref, sh_ref, sc_ref, w_ref, wlr_ref, z_ref, alr_ref, h_ref, p_ref, *,
                   blocks_per_batch):
    @pl.when(pl.program_id(1) == 0)
    def _():
        b = pl.program_id(0) // blocks_per_batch
        _norm_modulate(x_ref, g_ref, sh_ref[pl.ds(b, 1), :], sc_ref[pl.ds(b, 1), :], h_ref, p_ref)
        alr_ref[...] = jnp.dot(h_ref[...], wlr_ref[...], preferred_element_type=F32)

    z_ref[...] = jnp.dot(h_ref[...], w_ref[...], preferred_element_type=F32).astype(z_ref.dtype)


def _inproj(x, norm_g, mods, w_main, w_lr, n, layer, seq, tm=512, tn=1024):
    t, d = x.shape
    assert n % tn == 0 and n <= w_main.shape[-1]
    return pl.pallas_call(
        functools.partial(_inproj_kernel, blocks_per_batch=seq // tm),
        grid=(t // tm, n // tn),
        in_specs=[
            pl.BlockSpec((tm, d), lambda i, j: (i, 0)),
            pl.BlockSpec((None, 1, d), lambda i, j: (layer, 0, 0)),
            pl.BlockSpec((None, 8, d), lambda i, j: (layer, 0, 0)),
            pl.BlockSpec((None, 8, d), lambda i, j: (layer, 0, 1)),
            pl.BlockSpec((None, d, tn), lambda i, j: (layer, 0, j)),
            pl.BlockSpec((None, d, LANES), lambda i, j: (layer, 0, 0)),
        ],
        out_specs=[
            pl.BlockSpec((tm, tn), lambda i, j: (i, j)),
            pl.BlockSpec((tm, LANES), lambda i, j: (i, 0)),
        ],
        out_shape=[
            jax.ShapeDtypeStruct((t, n), F32),
            jax.ShapeDtypeStruct((t, LANES), F32),
        ],
        scratch_shapes=[pltpu.VMEM((tm, d), BF16), pltpu.VMEM((3, SUBLANES, d), F32)],
        compiler_params=_cparams(("parallel", "arbitrary")),
        name="in_proj",
    )(x, norm_g, mods, mods, w_main, w_lr)


def _conv_kernel(av_ref, ag_ref, hv_ref, hg_ref, cw_ref, cb_ref, lg_ref, lb_ref, o_ref, u_ref, y_ref, p_ref,
                 *, blocks_per_batch, lane_tile):
    n = pl.program_id(0)
    l = pl.program_id(1)
    ts = av_ref.shape[0]
    first = (n % blocks_per_batch) == 0

    hu = hv_ref[...] * _sigmoid(hg_ref[...])
    u_ref[0:CONV_HALO, :] = jnp.where(first, 0.0, hu)
    u_ref[CONV_HALO:, :] = av_ref[...] * _sigmoid(ag_ref[...])

    off = CONV_HALO - (CONV_WIDTH - 1)
    slab_rows = CHUNK + CONV_HALO
    for tt in range(ts // CHUNK):
        for lt in range(lane_tile // LANES):
            ls = slice(lt * LANES, (lt + 1) * LANES)
            slab = u_ref[tt * CHUNK:tt * CHUNK + slab_rows, ls]
            acc = jnp.broadcast_to(cb_ref[:, ls], (CHUNK, LANES))
            for res in range(8):
                taps = [w for w in range(CONV_WIDTH) if (off + w) % 8 == res]
                rot = slab if res == 0 else pltpu.roll(slab, slab_rows - res, axis=0)
                for w in taps:
                    a0 = (off + w) // 8 * 8
                    acc = acc + rot[a0:a0 + CHUNK, :] * cw_ref[w:w + 1, ls]
            y_ref[l, tt * CHUNK:(tt + 1) * CHUNK, ls] = acc

    @pl.when(l == pl.num_programs(1) - 1)
    def _():
        nl = y_ref.shape[0]
        c = nl * lane_tile
        rows = p_ref.shape[1]
        n = ts // rows

        def rows_of(r):
            return pl.ds(pl.multiple_of(jnp.minimum(r, n - 1) * rows, rows), rows)

        def mean_of(r):
            tot = y_ref[0, rows_of(r), :]
            for k in range(1, nl):
                tot = tot + y_ref[k, rows_of(r), :]
            return jnp.sum(tot, axis=-1, keepdims=True) / c

        def inv_std_of(r, mu):
            tot = jnp.square(y_ref[0, rows_of(r), :] - mu)
            for k in range(1, nl):
                tot = tot + jnp.square(y_ref[k, rows_of(r), :] - mu)
            return lax.rsqrt(jnp.sum(tot, axis=-1, keepdims=True) / c + NORM_EPS)

        def body(r, carry):
            mu0, inv0, mu1 = carry
            mu2 = mean_of(r + 2)
            inv1 = inv_std_of(r + 1, mu1)
            for k in range(nl):
                cs = slice(k * lane_tile, (k + 1) * lane_tile)
                yn = (y_ref[k, rows_of(r), :] - mu0) * inv0 * p_ref[0, :, cs] + p_ref[1, :, cs]
                o_ref[rows_of(r), cs] = _silu(yn).astype(o_ref.dtype)
            return mu1, inv1, mu2

        p_ref[0] = jnp.broadcast_to(lg_ref[...], (rows, c))
        p_ref[1] = jnp.broadcast_to(lb_ref[...], (rows, c))
        mu_first = mean_of(0)
        lax.fori_loop(0, n, body, (mu_first, inv_std_of(0, mu_first), mean_of(1)))


def _conv(z, conv_w, conv_b, ln_g, ln_b, layer, seq, conv_ch, ts=512, lane_tile=512):
    t = z.shape[0]
    nl = conv_ch // lane_tile
    hb = ts // CONV_HALO

    def halo_idx(col0):
        return lambda n, l: (jnp.maximum(n * hb - 1, 0), col0 + l)

    return pl.pallas_call(
        functools.partial(_conv_kernel, blocks_per_batch=seq // ts, lane_tile=lane_tile),
        grid=(t // ts, nl),
        in_specs=[
            pl.BlockSpec((ts, lane_tile), lambda n, l: (n, l)),
            pl.BlockSpec((ts, lane_tile), lambda n, l: (n, nl + l)),
            pl.BlockSpec((CONV_HALO, lane_tile), halo_idx(0)),
            pl.BlockSpec((CONV_HALO, lane_tile), halo_idx(nl)),
            pl.BlockSpec((None, CONV_HALO, lane_tile), lambda n, l: (layer, 0, l)),
            pl.BlockSpec((None, 1, lane_tile), lambda n, l: (layer, 0, l)),
            pl.BlockSpec((None, 1, conv_ch), lambda n, l: (layer, 0, 0)),
            pl.BlockSpec((None, 1, conv_ch), lambda n, l: (layer, 0, 0)),
        ],
        out_specs=pl.BlockSpec((ts, conv_ch), lambda n, l: (n, 0)),
        out_shape=jax.ShapeDtypeStruct((t, conv_ch), BF16),
        scratch_shapes=[
            pltpu.VMEM((CONV_HALO + ts, lane_tile), F32),
            pltpu.VMEM((nl, ts, lane_tile), F32),
            pltpu.VMEM((2, NORM_ROWS, conv_ch), F32),
        ],
        compiler_params=_cparams(("parallel", "arbitrary")),
        name="conformer_conv",
    )(z, z, z, z, conv_w, conv_b, ln_g, ln_b)


def _gla_kernel(q_ref, k_ref, v_ref, g_ref, alr_ref, wa_ref, ba_ref, ng_ref, o_ref, st_ref, *, q_scale):
    @pl.when(pl.program_id(2) == 0)
    def _():
        st_ref[...] = jnp.zeros_like(st_ref)

    rows, dk = q_ref.shape
    x = jnp.dot(alr_ref[...].astype(BF16), wa_ref[...], preferred_element_type=F32) + ba_ref[...]
    log_a = -(jnp.maximum(-x, 0.0) + jnp.log1p(jnp.exp(-jnp.abs(x)))) * (1.0 / GLA_TAU)
    sub = lax.broadcasted_iota(jnp.int32, (rows, dk), 0) & 7
    c = log_a
    for s in (1, 2, 4):
        c = c + jnp.where(sub >= s, pltpu.roll(c, s, axis=0), 0.0)

    tril = (lax.broadcasted_iota(jnp.int32, (CHUNK, CHUNK), 0)
            >= lax.broadcasted_iota(jnp.int32, (CHUNK, CHUNK), 1))
    contract_last = (((1,), (1,)), ((), ()))
    contract_first = (((0,), (0,)), ((), ()))

    for ci in range(rows // CHUNK):
        base = ci * CHUNK
        parts, carry = [], None
        for j in range(CHUNK // 8):
            blk = c[base + 8 * j:base + 8 * j + 8, :]
            if carry is not None:
                blk = blk + carry
            parts.append(blk)
            carry = blk[7:8, :]
        cum = jnp.concatenate(parts, axis=0)
        cl = carry
        rs = slice(base, base + CHUNK)
        kc = k_ref[rs, :]
        q_dec = ((q_ref[rs, :] * q_scale) * jnp.exp(cum)).astype(BF16)
        k_in = (kc * jnp.exp(-cum)).astype(BF16)
        k_st = (kc * jnp.exp(cl - cum)).astype(BF16)
        vb = v_ref[rs, :].astype(BF16)

        scores = lax.dot_general(q_dec, k_in, contract_last, preferred_element_type=F32)
        scores = jnp.where(tril, scores, 0.0).astype(BF16)
        st = st_ref[...]
        o = (jnp.dot(scores, vb, preferred_element_type=F32)
             + lax.dot_general(q_dec, st.astype(BF16), contract_last, preferred_element_type=F32))
        upd = lax.dot_general(vb, k_st, contract_first, preferred_element_type=F32)
        st_ref[...] = st * jnp.exp(cl) + upd

        ms = jnp.mean(o * o, axis=-1, keepdims=True)
        on = o * lax.rsqrt(ms + NORM_EPS) * ng_ref[...]
        o_ref[rs, :] = (on * _silu(g_ref[rs, :])).astype(o_ref.dtype)


def _gla(z, alr, w_alpha, b_alpha, norm_g, layer, batch, seq, conv_ch, dk, dv, rows=512):
    t = z.shape[0]
    nb = seq // rows
    q_col = 2 * conv_ch // dk
    k_col = q_col + GLA_HEADS
    v_col = (2 * conv_ch + 2 * GLA_HEADS * dk) // dv
    g_col = v_col + GLA_HEADS
    return pl.pallas_call(
        functools.partial(_gla_kernel, q_scale=float(dk) ** -0.5),
        grid=(batch, GLA_HEADS, nb),
        in_specs=[
            pl.BlockSpec((rows, dk), lambda b, h, n: (b * nb + n, q_col + h)),
            pl.BlockSpec((rows, dk), lambda b, h, n: (b * nb + n, k_col + h)),
            pl.BlockSpec((rows, dv), lambda b, h, n: (b * nb + n, v_col + h)),
            pl.BlockSpec((rows, dv), lambda b, h, n: (b * nb + n, g_col + h)),
            pl.BlockSpec((rows, LANES), lambda b, h, n: (b * nb + n, 0)),
            pl.BlockSpec((None, LANES, dk), lambda b, h, n: (layer, 0, h)),
            pl.BlockSpec((None, 1, dk), lambda b, h, n: (layer, 0, h)),
            pl.BlockSpec((None, 1, dv), lambda b, h, n: (layer, 0, 0)),
        ],
        out_specs=pl.BlockSpec((rows, dv), lambda b, h, n: (b * nb + n, h)),
        out_shape=jax.ShapeDtypeStruct((t, GLA_HEADS * dv), BF16),
        scratch_shapes=[pltpu.VMEM((dv, dk), F32)],
        compiler_params=_cparams(("parallel", "parallel", "arbitrary")),
        name="gla",
    )(z, z, z, z, alr, w_alpha, b_alpha, norm_g)


def _outproj_kernel(ya_ref, yb_ref, w_ref, x_ref, gt_ref, o_ref, *, blocks_per_batch):
    ca = ya_ref.shape[1]
    y = (jnp.dot(ya_ref[...], w_ref[0:ca, :], preferred_element_type=F32)
         + jnp.dot(yb_ref[...], w_ref[ca:, :], preferred_element_type=F32))
    b = pl.program_id(0) // blocks_per_batch
    o_ref[...] = x_ref[...] + gt_ref[pl.ds(b, 1), :] * y


def _outproj(ya, yb, w_out, x, mods, layer, seq, tm=1024, tn=1024):
    t, d = x.shape
    ca, cb = ya.shape[1], yb.shape[1]
    return pl.pallas_call(
        functools.partial(_outproj_kernel, blocks_per_batch=seq // tm),
        grid=(t // tm, d // tn),
        in_specs=[
            pl.BlockSpec((tm, ca), lambda i, j: (i, 0)),
            pl.BlockSpec((tm, cb), lambda i, j: (i, 0)),
            pl.BlockSpec((None, ca + cb, tn), lambda i, j: (layer, 0, j)),
            pl.BlockSpec((tm, tn), lambda i, j: (i, j)),
            pl.BlockSpec((None, 8, tn), lambda i, j: (layer, 0, 2 * (d // tn) + j)),
        ],
        out_specs=pl.BlockSpec((tm, tn), lambda i, j: (i, j)),
        out_shape=jax.ShapeDtypeStruct((t, d), F32),
        compiler_params=_cparams(("parallel", "arbitrary")),
        name="out_proj",
    )(ya, yb, w_out, x, mods)


def _mlp1_kernel(x_ref, g_ref, sh_ref, sc_ref, w_ref, o_ref, h_ref, p_ref, *, blocks_per_batch):
    @pl.when(pl.program_id(1) == 0)
    def _():
        b = pl.program_id(0) // blocks_per_batch
        _norm_modulate(x_ref, g_ref, sh_ref[pl.ds(b, 1), :], sc_ref[pl.ds(b, 1), :], h_ref, p_ref)

    a = jnp.maximum(jnp.dot(h_ref[...], w_ref[...], preferred_element_type=F32), 0.0)
    o_ref[...] = (a * a).astype(o_ref.dtype)


def _mlp1(x, norm_g, mods, w1, layer, seq, tm=512, tn=1024):
    t, d = x.shape
    n = w1.shape[-1]
    return pl.pallas_call(
        functools.partial(_mlp1_kernel, blocks_per_batch=seq // tm),
        grid=(t // tm, n // tn),
        in_specs=[
            pl.BlockSpec((tm, d), lambda i, j: (i, 0)),
            pl.BlockSpec((None, 1, d), lambda i, j: (layer, 0, 0)),
            pl.BlockSpec((None, 8, d), lambda i, j: (layer, 0, 3)),
            pl.BlockSpec((None, 8, d), lambda i, j: (layer, 0, 4)),
            pl.BlockSpec((None, d, tn), lambda i, j: (layer, 0, j)),
        ],
        out_specs=pl.BlockSpec((tm, tn), lambda i, j: (i, j)),
        out_shape=jax.ShapeDtypeStruct((t, n), BF16),
        scratch_shapes=[pltpu.VMEM((tm, d), BF16), pltpu.VMEM((3, SUBLANES, d), F32)],
        compiler_params=_cparams(("parallel", "arbitrary")),
        name="mlp_up",
    )(x, norm_g, mods, mods, w1)


def _mlp2_kernel(a_ref, w_ref, x_ref, gt_ref, o_ref, acc_ref, *, blocks_per_batch):
    kk = pl.program_id(2)

    @pl.when(kk == 0)
    def _():
        acc_ref[...] = jnp.zeros_like(acc_ref)

    acc_ref[...] += jnp.dot(a_ref[...], w_ref[...], preferred_element_type=F32)

    @pl.when(kk == pl.num_programs(2) - 1)
    def _():
        b = pl.program_id(0) // blocks_per_batch
        o_ref[...] = x_ref[...] + gt_ref[pl.ds(b, 1), :] * acc_ref[...]


def _mlp2(a, w2, x, mods, layer, seq, tm=1024, tn=512, tk=4096):
    t, d = x.shape
    kdim = a.shape[1]
    return pl.pallas_call(
        functools.partial(_mlp2_kernel, blocks_per_batch=seq // tm),
        grid=(t // tm, d // tn, kdim // tk),
        in_specs=[
            pl.BlockSpec((tm, tk), lambda i, j, k: (i, k)),
            pl.BlockSpec((None, tk, tn), lambda i, j, k: (layer, k, j)),
            pl.BlockSpec((tm, tn), lambda i, j, k: (i, j)),
            pl.BlockSpec((None, 8, tn), lambda i, j, k: (layer, 0, 5 * (d // tn) + j)),
        ],
        out_specs=pl.BlockSpec((tm, tn), lambda i, j, k: (i, j)),
        out_shape=jax.ShapeDtypeStruct((t, d), F32),
        scratch_shapes=[pltpu.VMEM((tm, tn), F32)],
        compiler_params=_cparams(("parallel", "parallel", "arbitrary")),
        name="mlp_down",
    )(a, w2, x, mods)


def _final_norm_kernel(x_ref, g_ref, o_ref):
    rows = 32
    g = g_ref[...]

    def body(r, carry):
        sl = pl.ds(pl.multiple_of(r * rows, rows), rows)
        xf = x_ref[sl, :]
        ms = jnp.mean(xf * xf, axis=-1, keepdims=True)
        o_ref[sl, :] = xf * lax.rsqrt(ms + NORM_EPS) * g
        return carry

    lax.fori_loop(0, x_ref.shape[0] // rows, body, 0)


def _final_norm(x, g, tm=512):
    t, d = x.shape
    return pl.pallas_call(
        _final_norm_kernel,
        grid=(t // tm,),
        in_specs=[pl.BlockSpec((tm, d), lambda i: (i, 0)), pl.BlockSpec((1, d), lambda i: (0, 0))],
        out_specs=pl.BlockSpec((tm, d), lambda i: (i, 0)),
        out_shape=jax.ShapeDtypeStruct((t, d), F32),
        compiler_params=_cparams(("parallel",)),
        name="final_norm",
    )(x, g)


def kernel(x, c, w_ada, b_ada, mod_table, norm1_g, w_in, conv_w, conv_b, conv_ln_g, conv_ln_b, w_alpha,
           b_alpha, gla_norm_g, w_out, norm2_g, w_mlp1, w_mlp2, final_g):
    batch, seq, d = x.shape
    depth = mod_table.shape[0]
    conv_ch = conv_w.shape[-1]
    dk = w_alpha.shape[-1] // GLA_HEADS
    dv = gla_norm_g.shape[-1]
    n_main = w_in.shape[-1] - GLA_LOWRANK
    assert batch <= 8 and mod_table.shape[1] == N_MOD and conv_w.shape[1] == CONV_WIDTH
    assert n_main == 2 * conv_ch + 2 * GLA_HEADS * (dk + dv) and w_alpha.shape[1] == GLA_LOWRANK

    w_main = w_in.astype(BF16)
    w_lr = jnp.pad(w_main[:, :, n_main:], ((0, 0), (0, 0), (0, LANES - GLA_LOWRANK)))
    wa = jnp.pad(w_alpha, ((0, 0), (0, LANES - GLA_LOWRANK), (0, 0))).astype(BF16)
    w_out_b = w_out.astype(BF16)
    w1 = w_mlp1.astype(BF16)
    w2 = w_mlp2.astype(BF16)
    cw = jnp.pad(conv_w, ((0, 0), (0, CONV_HALO - CONV_WIDTH), (0, 0)))
    c8 = jnp.pad(c.astype(F32), ((0, 8 - batch), (0, 0)))

    mods = _ada(c8, w_ada, b_ada.reshape(1, -1), mod_table.reshape(depth, 1, N_MOD * d))

    xf = x.reshape(batch * seq, d)
    for l in range(depth):
        z, alr = _inproj(xf, norm1_g.reshape(depth, 1, d), mods, w_main, w_lr, n_main, l, seq)
        y_a = _conv(z, cw, conv_b.reshape(depth, 1, conv_ch), conv_ln_g.reshape(depth, 1, conv_ch),
                    conv_ln_b.reshape(depth, 1, conv_ch), l, seq, conv_ch)
        y_b = _gla(z, alr, wa, b_alpha.reshape(depth, 1, -1), gla_norm_g.reshape(depth, 1, dv),
                   l, batch, seq, conv_ch, dk, dv)
        xf = _outproj(y_a, y_b, w_out_b, xf, mods, l, seq)
        hid = _mlp1(xf, norm2_g.reshape(depth, 1, d), mods, w1, l, seq)
        xf = _mlp2(hid, w2, xf, mods, l, seq)
    out = _final_norm(xf, final_g.reshape(1, d))
    return out.reshape(batch, seq, d)
```

```python
import functools

import jax
import jax.numpy as jnp
from jax import lax
from jax.experimental import pallas as pl
from jax.experimental.pallas import tpu as pltpu

F32 = jnp.float32
BF16 = jnp.bfloat16

CHUNK = 64
CONV_WIDTH = 31
CONV_HALO = 32
GLA_HEADS = 4
GLA_LOWRANK = 16
GLA_TAU = 16.0
N_MOD = 6
NORM_EPS = 1e-6
LANES = 128
SUBLANES = 8
NORM_ROWS = 16
VMEM_LIMIT = 56 * 1024 * 1024


def _cparams(sem):
    return pltpu.CompilerParams(dimension_semantics=sem, vmem_limit_bytes=VMEM_LIMIT)


def _sigmoid(x):
    return 1.0 / (1.0 + jnp.exp(-x))


def _silu(x):
    return x * _sigmoid(x)


def _ada_kernel(c_ref, w_ref, b_ref, t_ref, o_ref):
    s = _silu(c_ref[...]).astype(BF16)
    r = jnp.dot(s, w_ref[...].astype(BF16), preferred_element_type=F32) + b_ref[...]
    o_ref[...] = r[None, :, :] + t_ref[...]


def _ada(c8, w_ada, b_ada, table, tn=1024):
    d, n = w_ada.shape
    depth = table.shape[0]
    return pl.pallas_call(
        _ada_kernel,
        grid=(n // tn,),
        in_specs=[
            pl.BlockSpec((8, d), lambda j: (0, 0)),
            pl.BlockSpec((d, tn), lambda j: (0, j)),
            pl.BlockSpec((1, tn), lambda j: (0, j)),
            pl.BlockSpec((depth, 1, tn), lambda j: (0, 0, j)),
        ],
        out_specs=pl.BlockSpec((depth, 8, tn), lambda j: (0, 0, j)),
        out_shape=jax.ShapeDtypeStruct((depth, 8, n), F32),
        compiler_params=_cparams(("arbitrary",)),
        name="ada_mod",
    )(c8, w_ada, b_ada, table)


def _norm_kernel(x_ref, g_ref, sh_ref, sc_ref, h_ref, p_ref, *, blocks_per_batch):
    b = pl.program_id(0) // blocks_per_batch
    tm, d = x_ref.shape
    sub = p_ref.shape[1]
    groups = NORM_ROWS // sub
    n = tm // NORM_ROWS
    p_ref[0] = jnp.broadcast_to(g_ref[...], (sub, d))
    p_ref[1] = jnp.broadcast_to(1.0 + sc_ref[pl.ds(b, 1), :], (sub, d))
    p_ref[2] = jnp.broadcast_to(sh_ref[pl.ds(b, 1), :], (sub, d))

    def rows_of(r):
        return pl.ds(pl.multiple_of(r * NORM_ROWS, NORM_ROWS), NORM_ROWS)

    def sum_sq(r):
        xf = x_ref[rows_of(r), :].reshape(groups, sub, d)
        return jnp.sum(xf * xf, axis=-1, keepdims=True)

    def body(r, ss):
        ss_next = sum_sq(jnp.minimum(r + 1, n - 1))
        xf = x_ref[rows_of(r), :].reshape(groups, sub, d)
        y = xf * lax.rsqrt(ss * (1.0 / d) + NORM_EPS) * p_ref[0][None]
        h = y * p_ref[1][None] + p_ref[2][None]
        h_ref[rows_of(r), :] = h.reshape(NORM_ROWS, d).astype(BF16)
        return ss_next

    lax.fori_loop(0, n, body, sum_sq(0))


def _norm(x, norm_g, mods, layer, seq, shift_seg, tm=512):
    t, d = x.shape
    return pl.pallas_call(
        functools.partial(_norm_kernel, blocks_per_batch=seq // tm),
        grid=(t // tm,),
        in_specs=[
            pl.BlockSpec((tm, d), lambda i: (i, 0)),
            pl.BlockSpec((None, 1, d), lambda i: (layer, 0, 0)),
            pl.BlockSpec((None, 8, d), lambda i: (layer, 0, shift_seg)),
            pl.BlockSpec((None, 8, d), lambda i: (layer, 0, shift_seg + 1)),
        ],
        out_specs=pl.BlockSpec((tm, d), lambda i: (i, 0)),
        out_shape=jax.ShapeDtypeStruct((t, d), BF16),
        scratch_shapes=[pltpu.VMEM((3, SUBLANES, d), F32)],
        compiler_params=_cparams(("parallel",)),
        name="norm_mod",
    )(x, norm_g, mods, mods)


def _inproj_kernel(h_ref, w_ref, wlr_ref, z_ref, alr_ref):
    @pl.when(pl.program_id(1) == 0)
    def _():
        alr_ref[...] = jnp.dot(h_ref[...], wlr_ref[...], preferred_element_type=F32)

    z_ref[...] = jnp.dot(h_ref[...], w_ref[...], preferred_element_type=F32).astype(z_ref.dtype)


def _inproj(h, w_main, w_lr, n, layer, tm=1024, tn=1024):
    t, d = h.shape
    assert n % tn == 0 and n <= w_main.shape[-1]
    return pl.pallas_call(
        _inproj_kernel,
        grid=(t // tm, n // tn),
        in_specs=[
            pl.BlockSpec((tm, d), lambda i, j: (i, 0)),
            pl.BlockSpec((None, d, tn), lambda i, j: (layer, 0, j)),
            pl.BlockSpec((None, d, LANES), lambda i, j: (layer, 0, 0)),
        ],
        out_specs=[
            pl.BlockSpec((tm, tn), lambda i, j: (i, j)),
            pl.BlockSpec((tm, LANES), lambda i, j: (i, 0)),
        ],
        out_shape=[
            jax.ShapeDtypeStruct((t, n), F32),
            jax.ShapeDtypeStruct((t, LANES), F32),
        ],
        compiler_params=_cparams(("parallel", "arbitrary")),
        name="in_proj",
    )(h, w_main, w_lr)


def _conv_kernel(av_ref, ag_ref, hv_ref, hg_ref, cw_ref, cb_ref, lg_ref, lb_ref, o_ref, u_ref, y_ref, p_ref,
                 *, blocks_per_batch, lane_tile):
    n = pl.program_id(0)
    l = pl.program_id(1)
    ts = av_ref.shape[0]
    first = (n % blocks_per_batch) == 0

    hu = hv_ref[...] * _sigmoid(hg_ref[...])
    u_ref[0:CONV_HALO, :] = jnp.where(first, 0.0, hu)
    u_ref[CONV_HALO:, :] = av_ref[...] * _sigmoid(ag_ref[...])

    off = CONV_HALO - (CONV_WIDTH - 1)
    slab_rows = CHUNK + CONV_HALO
    for tt in range(ts // CHUNK):
        for lt in range(lane_tile // LANES):
            ls = slice(lt * LANES, (lt + 1) * LANES)
            slab = u_ref[tt * CHUNK:tt * CHUNK + slab_rows, ls]
            acc = jnp.broadcast_to(cb_ref[:, ls], (CHUNK, LANES))
            for res in range(8):
                taps = [w for w in range(CONV_WIDTH) if (off + w) % 8 == res]
                rot = slab if res == 0 else pltpu.roll(slab, slab_rows - res, axis=0)
                for w in taps:
                    a0 = (off + w) // 8 * 8
                    acc = acc + rot[a0:a0 + CHUNK, :] * cw_ref[w:w + 1, ls]
            y_ref[l, tt * CHUNK:(tt + 1) * CHUNK, ls] = acc

    @pl.when(l == pl.num_programs(1) - 1)
    def _():
        nl = y_ref.shape[0]
        c = nl * lane_tile
        rows = p_ref.shape[1]
        n = ts // rows

        def rows_of(r):
            return pl.ds(pl.multiple_of(jnp.minimum(r, n - 1) * rows, rows), rows)

        def mean_of(r):
            tot = y_ref[0, rows_of(r), :]
            for k in range(1, nl):
                tot = tot + y_ref[k, rows_of(r), :]
            return jnp.sum(tot, axis=-1, keepdims=True) / c

        def inv_std_of(r, mu):
            tot = jnp.square(y_ref[0, rows_of(r), :] - mu)
            for k in range(1, nl):
                tot = tot + jnp.square(y_ref[k, rows_of(r), :] - mu)
            return lax.rsqrt(jnp.sum(tot, axis=-1, keepdims=True) / c + NORM_EPS)

        def body(r, carry):
            mu0, inv0, mu1 = carry
            mu2 = mean_of(r + 2)
            inv1 = inv_std_of(r + 1, mu1)
            for k in range(nl):
                cs = slice(k * lane_tile, (k + 1) * lane_tile)
                yn = (y_ref[k, rows_of(r), :] - mu0) * inv0 * p_ref[0, :, cs] + p_ref[1, :, cs]
                o_ref[rows_of(r), cs] = _silu(yn).astype(o_ref.dtype)
            return mu1, inv1, mu2

        p_ref[0] = jnp.broadcast_to(lg_ref[...], (rows, c))
        p_ref[1] = jnp.broadcast_to(lb_ref[...], (rows, c))
        mu_first = mean_of(0)
        lax.fori_loop(0, n, body, (mu_first, inv_std_of(0, mu_first), mean_of(1)))


def _conv(z, conv_w, conv_b, ln_g, ln_b, layer, seq, conv_ch, ts=512, lane_tile=512):
    t = z.shape[0]
    nl = conv_ch // lane_tile
    hb = ts // CONV_HALO

    def halo_idx(col0):
        return lambda n, l: (jnp.maximum(n * hb - 1, 0), col0 + l)

    return pl.pallas_call(
        functools.partial(_conv_kernel, blocks_per_batch=seq // ts, lane_tile=lane_tile),
        grid=(t // ts, nl),
        in_specs=[
            pl.BlockSpec((ts, lane_tile), lambda n, l: (n, l)),
            pl.BlockSpec((ts, lane_tile), lambda n, l: (n, nl + l)),
            pl.BlockSpec((CONV_HALO, lane_tile), halo_idx(0)),
            pl.BlockSpec((CONV_HALO, lane_tile), halo_idx(nl)),
            pl.BlockSpec((None, CONV_HALO, lane_tile), lambda n, l: (layer, 0, l)),
            pl.BlockSpec((None, 1, lane_tile), lambda n, l: (layer, 0, l)),
            pl.BlockSpec((None, 1, conv_ch), lambda n, l: (layer, 0, 0)),
            pl.BlockSpec((None, 1, conv_ch), lambda n, l: (layer, 0, 0)),
        ],
        out_specs=pl.BlockSpec((ts, conv_ch), lambda n, l: (n, 0)),
        out_shape=jax.ShapeDtypeStruct((t, conv_ch), BF16),
        scratch_shapes=[
            pltpu.VMEM((CONV_HALO + ts, lane_tile), F32),
            pltpu.VMEM((nl, ts, lane_tile), F32),
            pltpu.VMEM((2, NORM_ROWS, conv_ch), F32),
        ],
        compiler_params=_cparams(("parallel", "arbitrary")),
        name="conformer_conv",
    )(z, z, z, z, conv_w, conv_b, ln_g, ln_b)


def _gla_kernel(q_ref, k_ref, v_ref, g_ref, alr_ref, wa_ref, ba_ref, ng_ref, o_ref, st_ref, *, q_scale):
    @pl.when(pl.program_id(2) == 0)
    def _():
        st_ref[...] = jnp.zeros_like(st_ref)

    heads, dv, dk = st_ref.shape
    rows = q_ref.shape[0]
    x = jnp.dot(alr_ref[...].astype(BF16), wa_ref[...], preferred_element_type=F32) + ba_ref[...]
    log_a = -(jnp.maximum(-x, 0.0) + jnp.log1p(jnp.exp(-jnp.abs(x)))) * (1.0 / GLA_TAU)
    sub = lax.broadcasted_iota(jnp.int32, (rows, heads * dk), 0) & 7
    c = log_a
    for s in (1, 2, 4):
        c = c + jnp.where(sub >= s, pltpu.roll(c, s, axis=0), 0.0)

    tril = (lax.broadcasted_iota(jnp.int32, (CHUNK, CHUNK), 0)
            >= lax.broadcasted_iota(jnp.int32, (CHUNK, CHUNK), 1))
    contract_last = (((1,), (1,)), ((), ()))
    contract_first = (((0,), (0,)), ((), ()))

    for ci in range(rows // CHUNK):
        base = ci * CHUNK
        parts, carry = [], None
        for j in range(CHUNK // 8):
            blk = c[base + 8 * j:base + 8 * j + 8, :]
            if carry is not None:
                blk = blk + carry
            parts.append(blk)
            carry = blk[7:8, :]
        cum_all = jnp.concatenate(parts, axis=0)
        rs = slice(base, base + CHUNK)
        for hh in range(heads):
            ks = slice(hh * dk, (hh + 1) * dk)
            vs = slice(hh * dv, (hh + 1) * dv)
            cum = cum_all[:, ks]
            cl = carry[:, ks]
            kc = k_ref[rs, ks]
            q_dec = ((q_ref[rs, ks] * q_scale) * jnp.exp(cum)).astype(BF16)
            k_in = (kc * jnp.exp(-cum)).astype(BF16)
            k_st = (kc * jnp.exp(cl - cum)).astype(BF16)
            vb = v_ref[rs, vs].astype(BF16)

            scores = lax.dot_general(q_dec, k_in, contract_last, preferred_element_type=F32)
            scores = jnp.where(tril, scores, 0.0).astype(BF16)
            st = st_ref[hh]
            o = (jnp.dot(scores, vb, preferred_element_type=F32)
                 + lax.dot_general(q_dec, st.astype(BF16), contract_last, preferred_element_type=F32))
            upd = lax.dot_general(vb, k_st, contract_first, preferred_element_type=F32)
            st_ref[hh] = st * jnp.exp(cl) + upd

            ms = jnp.mean(o * o, axis=-1, keepdims=True)
            on = o * lax.rsqrt(ms + NORM_EPS) * ng_ref[...]
            o_ref[rs, vs] = (on * _silu(g_ref[rs, vs])).astype(o_ref.dtype)


def _gla(z, alr, w_alpha, b_alpha, norm_g, layer, batch, seq, conv_ch, dk, dv, rows=512, heads=2):
    t = z.shape[0]
    nb = seq // rows
    hg = GLA_HEADS // heads
    wk, wv = heads * dk, heads * dv
    q_col = 2 * conv_ch // wk
    k_col = q_col + hg
    v_col = (2 * conv_ch + 2 * GLA_HEADS * dk) // wv
    g_col = v_col + hg
    return pl.pallas_call(
        functools.partial(_gla_kernel, q_scale=float(dk) ** -0.5),
        grid=(batch, hg, nb),
        in_specs=[
            pl.BlockSpec((rows, wk), lambda b, h, n: (b * nb + n, q_col + h)),
            pl.BlockSpec((rows, wk), lambda b, h, n: (b * nb + n, k_col + h)),
            pl.BlockSpec((rows, wv), lambda b, h, n: (b * nb + n, v_col + h)),
            pl.BlockSpec((rows, wv), lambda b, h, n: (b * nb + n, g_col + h)),
            pl.BlockSpec((rows, LANES), lambda b, h, n: (b * nb + n, 0)),
            pl.BlockSpec((None, LANES, wk), lambda b, h, n: (layer, 0, h)),
            pl.BlockSpec((None, 1, wk), lambda b, h, n: (layer, 0, h)),
            pl.BlockSpec((None, 1, dv), lambda b, h, n: (layer, 0, 0)),
        ],
        out_specs=pl.BlockSpec((rows, wv), lambda b, h, n: (b * nb + n, h)),
        out_shape=jax.ShapeDtypeStruct((t, GLA_HEADS * dv), BF16),
        scratch_shapes=[pltpu.VMEM((heads, dv, dk), F32)],
        compiler_params=_cparams(("parallel", "parallel", "arbitrary")),
        name="gla",
    )(z, z, z, z, alr, w_alpha, b_alpha, norm_g)


def _outproj_kernel(ya_ref, yb_ref, w_ref, x_ref, gt_ref, o_ref, *, blocks_per_batch):
    ca = ya_ref.shape[1]
    y = (jnp.dot(ya_ref[...], w_ref[0:ca, :], preferred_element_type=F32)
         + jnp.dot(yb_ref[...], w_ref[ca:, :], preferred_element_type=F32))
    b = pl.program_id(0) // blocks_per_batch
    o_ref[...] = x_ref[...] + gt_ref[pl.ds(b, 1), :] * y


def _outproj(ya, yb, w_out, x, mods, layer, seq, tm=1024, tn=1024):
    t, d = x.shape
    ca, cb = ya.shape[1], yb.shape[1]
    return pl.pallas_call(
        functools.partial(_outproj_kernel, blocks_per_batch=seq // tm),
        grid=(t // tm, d // tn),
        in_specs=[
            pl.BlockSpec((tm, ca), lambda i, j: (i, 0)),
            pl.BlockSpec((tm, cb), lambda i, j: (i, 0)),
            pl.BlockSpec((None, ca + cb, tn), lambda i, j: (layer, 0, j)),
            pl.BlockSpec((tm, tn), lambda i, j: (i, j)),
            pl.BlockSpec((None, 8, tn), lambda i, j: (layer, 0, 2 * (d // tn) + j)),
        ],
        out_specs=pl.BlockSpec((tm, tn), lambda i, j: (i, j)),
        out_shape=jax.ShapeDtypeStruct((t, d), F32),
        compiler_params=_cparams(("parallel", "arbitrary")),
        name="out_proj",
    )(ya, yb, w_out, x, mods)


def _mlp1_kernel(h_ref, w_ref, o_ref):
    a = jnp.maximum(jnp.dot(h_ref[...], w_ref[...], preferred_element_type=F32), 0.0)
    o_ref[...] = (a * a).astype(o_ref.dtype)


def _mlp1(h, w1, layer, tm=1024, tn=1024):
    t, d = h.shape
    n = w1.shape[-1]
    return pl.pallas_call(
        _mlp1_kernel,
        grid=(t // tm, n // tn),
        in_specs=[
            pl.BlockSpec((tm, d), lambda i, j: (i, 0)),
            pl.BlockSpec((None, d, tn), lambda i, j: (layer, 0, j)),
        ],
        out_specs=pl.BlockSpec((tm, tn), lambda i, j: (i, j)),
        out_shape=jax.ShapeDtypeStruct((t, n), BF16),
        compiler_params=_cparams(("parallel", "arbitrary")),
        name="mlp_up",
    )(h, w1)


def _mlp2_kernel(a_ref, w_ref, x_ref, gt_ref, o_ref, *, blocks_per_batch):
    kk = pl.program_id(2)

    @pl.when(kk == 0)
    def _():
        o_ref[...] = jnp.zeros_like(o_ref)

    o_ref[...] += jnp.dot(a_ref[...], w_ref[...], preferred_element_type=F32)

    @pl.when(kk == pl.num_programs(2) - 1)
    def _():
        b = pl.program_id(0) // blocks_per_batch
        o_ref[...] = x_ref[...] + gt_ref[pl.ds(b, 1), :] * o_ref[...]


def _mlp2(a, w2, x, mods, layer, seq, tm=1024, tn=1024, tk=4096):
    t, d = x.shape
    kdim = a.shape[1]
    return pl.pallas_call(
        functools.partial(_mlp2_kernel, blocks_per_batch=seq // tm),
        grid=(t // tm, d // tn, kdim // tk),
        in_specs=[
            pl.BlockSpec((tm, tk), lambda i, j, k: (i, k)),
            pl.BlockSpec((None, tk, tn), lambda i, j, k: (layer, k, j)),
            pl.BlockSpec((tm, tn), lambda i, j, k: (i, j)),
            pl.BlockSpec((None, 8, tn), lambda i, j, k: (layer, 0, 5 * (d // tn) + j)),
        ],
        out_specs=pl.BlockSpec((tm, tn), lambda i, j, k: (i, j)),
        out_shape=jax.ShapeDtypeStruct((t, d), F32),
        compiler_params=_cparams(("parallel", "parallel", "arbitrary")),
        name="mlp_down",
    )(a, w2, x, mods)


def _final_norm_kernel(x_ref, g_ref, o_ref):
    rows = 32
    g = g_ref[...]

    def body(r, carry):
        sl = pl.ds(pl.multiple_of(r * rows, rows), rows)
        xf = x_ref[sl, :]
        ms = jnp.mean(xf * xf, axis=-1, keepdims=True)
        o_ref[sl, :] = xf * lax.rsqrt(ms + NORM_EPS) * g
        return carry

    lax.fori_loop(0, x_ref.shape[0] // rows, body, 0)


def _final_norm(x, g, tm=512):
    t, d = x.shape
    return pl.pallas_call(
        _final_norm_kernel,
        grid=(t // tm,),
        in_specs=[pl.BlockSpec((tm, d), lambda i: (i, 0)), pl.BlockSpec((1, d), lambda i: (0, 0))],
        out_specs=pl.BlockSpec((tm, d), lambda i: (i, 0)),
        out_shape=jax.ShapeDtypeStruct((t, d), F32),
        compiler_params=_cparams(("parallel",)),
        name="final_norm",
    )(x, g)


def kernel(x, c, w_ada, b_ada, mod_table, norm1_g, w_in, conv_w, conv_b, conv_ln_g, conv_ln_b, w_alpha,
           b_alpha, gla_norm_g, w_out, norm2_g, w_mlp1, w_mlp2, final_g):
    batch, seq, d = x.shape
    depth = mod_table.shape[0]
    conv_ch = conv_w.shape[-1]
    dk = w_alpha.shape[-1] // GLA_HEADS
    dv = gla_norm_g.shape[-1]
    n_main = w_in.shape[-1] - GLA_LOWRANK
    assert batch <= 8 and mod_table.shape[1] == N_MOD and conv_w.shape[1] == CONV_WIDTH
    assert n_main == 2 * conv_ch + 2 * GLA_HEADS * (dk + dv) and w_alpha.shape[1] == GLA_LOWRANK

    w_main = w_in.astype(BF16)
    w_lr = jnp.pad(w_main[:, :, n_main:], ((0, 0), (0, 0), (0, LANES - GLA_LOWRANK)))
    wa = jnp.pad(w_alpha, ((0, 0), (0, LANES - GLA_LOWRANK), (0, 0))).astype(BF16)
    w_out_b = w_out.astype(BF16)
    w1 = w_mlp1.astype(BF16)
    w2 = w_mlp2.astype(BF16)
    cw = jnp.pad(conv_w, ((0, 0), (0, CONV_HALO - CONV_WIDTH), (0, 0)))
    c8 = jnp.pad(c.astype(F32), ((0, 8 - batch), (0, 0)))

    mods = _ada(c8, w_ada, b_ada.reshape(1, -1), mod_table.reshape(depth, 1, N_MOD * d))

    xf = x.reshape(batch * seq, d)
    for l in range(depth):
        h = _norm(xf, norm1_g.reshape(depth, 1, d), mods, l, seq, 0)
        z, alr = _inproj(h, w_main, w_lr, n_main, l)
        y_a = _conv(z, cw, conv_b.reshape(depth, 1, conv_ch), conv_ln_g.reshape(depth, 1, conv_ch),
                    conv_ln_b.reshape(depth, 1, conv_ch), l, seq, conv_ch)
        y_b = _gla(z, alr, wa, b_alpha.reshape(depth, 1, -1), gla_norm_g.reshape(depth, 1, dv),
                   l, batch, seq, conv_ch, dk, dv)
        xf = _outproj(y_a, y_b, w_out_b, xf, mods, l, seq)
        h = _norm(xf, norm2_g.reshape(depth, 1, d), mods, l, seq, 3)
        hid = _mlp1(h, w1, l)
        xf = _mlp2(hid, w2, xf, mods, l, seq)
    out = _final_norm(xf, final_g.reshape(1, d))
    return out.reshape(batch, seq, d)
```

```python
import functools

import jax
import jax.numpy as jnp
from jax import lax
from jax.experimental import pallas as pl
from jax.experimental.pallas import tpu as pltpu

F32 = jnp.float32
BF16 = jnp.bfloat16

CHUNK = 64
CONV_WIDTH = 31
CONV_HALO = 32
GLA_HEADS = 4
GLA_LOWRANK = 16
GLA_TAU = 16.0
N_MOD = 6
NORM_EPS = 1e-6
LANES = 128
SUBLANES = 8
NORM_ROWS = 16
VMEM_LIMIT = 56 * 1024 * 1024


def _cparams(sem):
    return pltpu.CompilerParams(dimension_semantics=sem, vmem_limit_bytes=VMEM_LIMIT)


def _sigmoid(x):
    return 1.0 / (1.0 + jnp.exp(-x))


def _silu(x):
    return x * _sigmoid(x)


def _ada_kernel(c_ref, w_ref, b_ref, t_ref, o_ref):
    s = _silu(c_ref[...]).astype(BF16)
    r = jnp.dot(s, w_ref[...].astype(BF16), preferred_element_type=F32) + b_ref[...]
    o_ref[...] = r[None, :, :] + t_ref[...]


def _ada(c8, w_ada, b_ada, table, tn=1024):
    d, n = w_ada.shape
    depth = table.shape[0]
    return pl.pallas_call(
        _ada_kernel,
        grid=(n // tn,),
        in_specs=[
            pl.BlockSpec((8, d), lambda j: (0, 0)),
            pl.BlockSpec((d, tn), lambda j: (0, j)),
            pl.BlockSpec((1, tn), lambda j: (0, j)),
            pl.BlockSpec((depth, 1, tn), lambda j: (0, 0, j)),
        ],
        out_specs=pl.BlockSpec((depth, 8, tn), lambda j: (0, 0, j)),
        out_shape=jax.ShapeDtypeStruct((depth, 8, n), F32),
        compiler_params=_cparams(("arbitrary",)),
        name="ada_mod",
    )(c8, w_ada, b_ada, table)


def _norm_kernel(x_ref, g_ref, sh_ref, sc_ref, h_ref, p_ref, *, blocks_per_batch):
    b = pl.program_id(0) // blocks_per_batch
    tm, d = x_ref.shape
    sub = p_ref.shape[1]
    groups = NORM_ROWS // sub
    n = tm // NORM_ROWS
    p_ref[0] = jnp.broadcast_to(g_ref[...], (sub, d))
    p_ref[1] = jnp.broadcast_to(1.0 + sc_ref[pl.ds(b, 1), :], (sub, d))
    p_ref[2] = jnp.broadcast_to(sh_ref[pl.ds(b, 1), :], (sub, d))

    def rows_of(r):
        return pl.ds(pl.multiple_of(r * NORM_ROWS, NORM_ROWS), NORM_ROWS)

    def sum_sq(r):
        xf = x_ref[rows_of(r), :].reshape(groups, sub, d)
        return jnp.sum(xf * xf, axis=-1, keepdims=True)

    def body(r, ss):
        ss_next = sum_sq(jnp.minimum(r + 1, n - 1))
        xf = x_ref[rows_of(r), :].reshape(groups, sub, d)
        y = xf * lax.rsqrt(ss * (1.0 / d) + NORM_EPS) * p_ref[0][None]
        h = y * p_ref[1][None] + p_ref[2][None]
        h_ref[rows_of(r), :] = h.reshape(NORM_ROWS, d).astype(BF16)
        return ss_next

    lax.fori_loop(0, n, body, sum_sq(0))


def _norm(x, norm_g, mods, layer, seq, shift_seg, tm=512):
    t, d = x.shape
    return pl.pallas_call(
        functools.partial(_norm_kernel, blocks_per_batch=seq // tm),
        grid=(t // tm,),
        in_specs=[
            pl.BlockSpec((tm, d), lambda i: (i, 0)),
            pl.BlockSpec((None, 1, d), lambda i: (layer, 0, 0)),
            pl.BlockSpec((None, 8, d), lambda i: (layer, 0, shift_seg)),
            pl.BlockSpec((None, 8, d), lambda i: (layer, 0, shift_seg + 1)),
        ],
        out_specs=pl.BlockSpec((tm, d), lambda i: (i, 0)),
        out_shape=jax.ShapeDtypeStruct((t, d), BF16),
        scratch_shapes=[pltpu.VMEM((3, SUBLANES, d), F32)],
        compiler_params=_cparams(("parallel",)),
        name="norm_mod",
    )(x, norm_g, mods, mods)


def _inproj_kernel(h_ref, w_ref, wlr_ref, z_ref, alr_ref):
    @pl.when(pl.program_id(1) == 0)
    def _():
        alr_ref[...] = jnp.dot(h_ref[...], wlr_ref[...], preferred_element_type=F32)

    z_ref[...] = jnp.dot(h_ref[...], w_ref[...], preferred_element_type=F32).astype(z_ref.dtype)


def _inproj(h, w_main, w_lr, n, tm=1024, tn=1024):
    t, d = h.shape
    assert n % tn == 0 and n <= w_main.shape[-1]
    return pl.pallas_call(
        _inproj_kernel,
        grid=(t // tm, n // tn),
        in_specs=[
            pl.BlockSpec((tm, d), lambda i, j: (i, 0)),
            pl.BlockSpec((d, tn), lambda i, j: (0, j)),
            pl.BlockSpec((d, LANES), lambda i, j: (0, 0)),
        ],
        out_specs=[
            pl.BlockSpec((tm, tn), lambda i, j: (i, j)),
            pl.BlockSpec((tm, LANES), lambda i, j: (i, 0)),
        ],
        out_shape=[
            jax.ShapeDtypeStruct((t, n), F32),
            jax.ShapeDtypeStruct((t, LANES), F32),
        ],
        compiler_params=_cparams(("parallel", "arbitrary")),
        name="in_proj",
    )(h, w_main, w_lr)


def _conv_kernel(av_ref, ag_ref, hv_ref, hg_ref, cw_ref, cb_ref, lg_ref, lb_ref, o_ref, u_ref, y_ref, p_ref,
                 *, blocks_per_batch, lane_tile):
    n = pl.program_id(0)
    l = pl.program_id(1)
    ts = av_ref.shape[0]
    first = (n % blocks_per_batch) == 0

    hu = hv_ref[...] * _sigmoid(hg_ref[...])
    u_ref[0:CONV_HALO, :] = jnp.where(first, 0.0, hu)
    u_ref[CONV_HALO:, :] = av_ref[...] * _sigmoid(ag_ref[...])

    off = CONV_HALO - (CONV_WIDTH - 1)
    slab_rows = CHUNK + CONV_HALO
    for tt in range(ts // CHUNK):
        for lt in range(lane_tile // LANES):
            ls = slice(lt * LANES, (lt + 1) * LANES)
            slab = u_ref[tt * CHUNK:tt * CHUNK + slab_rows, ls]
            acc = jnp.broadcast_to(cb_ref[:, ls], (CHUNK, LANES))
            for res in range(8):
                taps = [w for w in range(CONV_WIDTH) if (off + w) % 8 == res]
                rot = slab if res == 0 else pltpu.roll(slab, slab_rows - res, axis=0)
                for w in taps:
                    a0 = (off + w) // 8 * 8
                    acc = acc + rot[a0:a0 + CHUNK, :] * cw_ref[w:w + 1, ls]
            y_ref[l, tt * CHUNK:(tt + 1) * CHUNK, ls] = acc

    @pl.when(l == pl.num_programs(1) - 1)
    def _():
        nl = y_ref.shape[0]
        c = nl * lane_tile
        rows = p_ref.shape[1]
        n = ts // rows

        def rows_of(r):
            return pl.ds(pl.multiple_of(jnp.minimum(r, n - 1) * rows, rows), rows)

        def mean_of(r):
            tot = y_ref[0, rows_of(r), :]
            for k in range(1, nl):
                tot = tot + y_ref[k, rows_of(r), :]
            return jnp.sum(tot, axis=-1, keepdims=True) / c

        def inv_std_of(r, mu):
            tot = jnp.square(y_ref[0, rows_of(r), :] - mu)
            for k in range(1, nl):
                tot = tot + jnp.square(y_ref[k, rows_of(r), :] - mu)
            return lax.rsqrt(jnp.sum(tot, axis=-1, keepdims=True) / c + NORM_EPS)

        def body(r, carry):
            mu0, inv0, mu1 = carry
            mu2 = mean_of(r + 2)
            inv1 = inv_std_of(r + 1, mu1)
            for k in range(nl):
                cs = slice(k * lane_tile, (k + 1) * lane_tile)
                yn = (y_ref[k, rows_of(r), :] - mu0) * inv0 * p_ref[0, :, cs] + p_ref[1, :, cs]
                o_ref[rows_of(r), cs] = _silu(yn).astype(o_ref.dtype)
            return mu1, inv1, mu2

        p_ref[0] = jnp.broadcast_to(lg_ref[...], (rows, c))
        p_ref[1] = jnp.broadcast_to(lb_ref[...], (rows, c))
        mu_first = mean_of(0)
        lax.fori_loop(0, n, body, (mu_first, inv_std_of(0, mu_first), mean_of(1)))


def _conv(z, conv_w, conv_b, ln_g, ln_b, layer, seq, conv_ch, ts=512, lane_tile=512):
    t = z.shape[0]
    nl = conv_ch // lane_tile
    hb = ts // CONV_HALO

    def halo_idx(col0):
        return lambda n, l: (jnp.maximum(n * hb - 1, 0), col0 + l)

    return pl.pallas_call(
        functools.partial(_conv_kernel, blocks_per_batch=seq // ts, lane_tile=lane_tile),
        grid=(t // ts, nl),
        in_specs=[
            pl.BlockSpec((ts, lane_tile), lambda n, l: (n, l)),
            pl.BlockSpec((ts, lane_tile), lambda n, l: (n, nl + l)),
            pl.BlockSpec((CONV_HALO, lane_tile), halo_idx(0)),
            pl.BlockSpec((CONV_HALO, lane_tile), halo_idx(nl)),
            pl.BlockSpec((None, CONV_HALO, lane_tile), lambda n, l: (layer, 0, l)),
            pl.BlockSpec((None, 1, lane_tile), lambda n, l: (layer, 0, l)),
            pl.BlockSpec((None, 1, conv_ch), lambda n, l: (layer, 0, 0)),
            pl.BlockSpec((None, 1, conv_ch), lambda n, l: (layer, 0, 0)),
        ],
        out_specs=pl.BlockSpec((ts, conv_ch), lambda n, l: (n, 0)),
        out_shape=jax.ShapeDtypeStruct((t, conv_ch), BF16),
        scratch_shapes=[
            pltpu.VMEM((CONV_HALO + ts, lane_tile), F32),
            pltpu.VMEM((nl, ts, lane_tile), F32),
            pltpu.VMEM((2, NORM_ROWS, conv_ch), F32),
        ],
        compiler_params=_cparams(("parallel", "arbitrary")),
        name="conformer_conv",
    )(z, z, z, z, conv_w, conv_b, ln_g, ln_b)


def _gla_kernel(q_ref, k_ref, v_ref, g_ref, alr_ref, wa_ref, ba_ref, ng_ref, o_ref, st_ref, *, q_scale):
    @pl.when(pl.program_id(2) == 0)
    def _():
        st_ref[...] = jnp.zeros_like(st_ref)

    heads, dv, dk = st_ref.shape
    rows = q_ref.shape[0]
    x = jnp.dot(alr_ref[...].astype(BF16), wa_ref[...], preferred_element_type=F32) + ba_ref[...]
    log_a = -(jnp.maximum(-x, 0.0) + jnp.log1p(jnp.exp(-jnp.abs(x)))) * (1.0 / GLA_TAU)
    sub = lax.broadcasted_iota(jnp.int32, (rows, heads * dk), 0) & 7
    c = log_a
    for s in (1, 2, 4):
        c = c + jnp.where(sub >= s, pltpu.roll(c, s, axis=0), 0.0)

    tril = (lax.broadcasted_iota(jnp.int32, (CHUNK, CHUNK), 0)
            >= lax.broadcasted_iota(jnp.int32, (CHUNK, CHUNK), 1))
    contract_last = (((1,), (1,)), ((), ()))
    contract_first = (((0,), (0,)), ((), ()))

    for ci in range(rows // CHUNK):
        base = ci * CHUNK
        parts, carry = [], None
        for j in range(CHUNK // 8):
            blk = c[base + 8 * j:base + 8 * j + 8, :]
            if carry is not None:
                blk = blk + carry
            parts.append(blk)
            carry = blk[7:8, :]
        cum_all = jnp.concatenate(parts, axis=0)
        rs = slice(base, base + CHUNK)
        for hh in range(heads):
            ks = slice(hh * dk, (hh + 1) * dk)
            vs = slice(hh * dv, (hh + 1) * dv)
            cum = cum_all[:, ks]
            cl = carry[:, ks]
            kc = k_ref[rs, ks]
            q_dec = ((q_ref[rs, ks] * q_scale) * jnp.exp(cum)).astype(BF16)
            k_in = (kc * jnp.exp(-cum)).astype(BF16)
            k_st = (kc * jnp.exp(cl - cum)).astype(BF16)
            vb = v_ref[rs, vs].astype(BF16)

            scores = lax.dot_general(q_dec, k_in, contract_last, preferred_element_type=F32)
            scores = jnp.where(tril, scores, 0.0).astype(BF16)
            st = st_ref[hh]
            o = (jnp.dot(scores, vb, preferred_element_type=F32)
                 + lax.dot_general(q_dec, st.astype(BF16), contract_last, preferred_element_type=F32))
            upd = lax.dot_general(vb, k_st, contract_first, preferred_element_type=F32)
            st_ref[hh] = st * jnp.exp(cl) + upd

            ms = jnp.mean(o * o, axis=-1, keepdims=True)
            on = o * lax.rsqrt(ms + NORM_EPS) * ng_ref[...]
            o_ref[rs, vs] = (on * _silu(g_ref[rs, vs])).astype(o_ref.dtype)


def _gla(z, alr, w_alpha, b_alpha, norm_g, layer, batch, seq, conv_ch, dk, dv, rows=512, heads=2):
    t = z.shape[0]
    nb = seq // rows
    hg = GLA_HEADS // heads
    wk, wv = heads * dk, heads * dv
    q_col = 2 * conv_ch // wk
    k_col = q_col + hg
    v_col = (2 * conv_ch + 2 * GLA_HEADS * dk) // wv
    g_col = v_col + hg
    return pl.pallas_call(
        functools.partial(_gla_kernel, q_scale=float(dk) ** -0.5),
        grid=(batch, hg, nb),
        in_specs=[
            pl.BlockSpec((rows, wk), lambda b, h, n: (b * nb + n, q_col + h)),
            pl.BlockSpec((rows, wk), lambda b, h, n: (b * nb + n, k_col + h)),
            pl.BlockSpec((rows, wv), lambda b, h, n: (b * nb + n, v_col + h)),
            pl.BlockSpec((rows, wv), lambda b, h, n: (b * nb + n, g_col + h)),
            pl.BlockSpec((rows, LANES), lambda b, h, n: (b * nb + n, 0)),
            pl.BlockSpec((None, LANES, wk), lambda b, h, n: (layer, 0, h)),
            pl.BlockSpec((None, 1, wk), lambda b, h, n: (layer, 0, h)),
            pl.BlockSpec((None, 1, dv), lambda b, h, n: (layer, 0, 0)),
        ],
        out_specs=pl.BlockSpec((rows, wv), lambda b, h, n: (b * nb + n, h)),
        out_shape=jax.ShapeDtypeStruct((t, GLA_HEADS * dv), BF16),
        scratch_shapes=[pltpu.VMEM((heads, dv, dk), F32)],
        compiler_params=_cparams(("parallel", "parallel", "arbitrary")),
        name="gla",
    )(z, z, z, z, alr, w_alpha, b_alpha, norm_g)


def _outproj_kernel(ya_ref, yb_ref, w_ref, x_ref, gt_ref, o_ref, *, blocks_per_batch):
    ca = ya_ref.shape[1]
    y = (jnp.dot(ya_ref[...], w_ref[0:ca, :], preferred_element_type=F32)
         + jnp.dot(yb_ref[...], w_ref[ca:, :], preferred_element_type=F32))
    b = pl.program_id(0) // blocks_per_batch
    o_ref[...] = x_ref[...] + gt_ref[pl.ds(b, 1), :] * y


def _outproj(ya, yb, w_out, x, mods, layer, seq, tm=1024, tn=1024):
    t, d = x.shape
    ca, cb = ya.shape[1], yb.shape[1]
    return pl.pallas_call(
        functools.partial(_outproj_kernel, blocks_per_batch=seq // tm),
        grid=(t // tm, d // tn),
        in_specs=[
            pl.BlockSpec((tm, ca), lambda i, j: (i, 0)),
            pl.BlockSpec((tm, cb), lambda i, j: (i, 0)),
            pl.BlockSpec((ca + cb, tn), lambda i, j: (0, j)),
            pl.BlockSpec((tm, tn), lambda i, j: (i, j)),
            pl.BlockSpec((None, 8, tn), lambda i, j: (layer, 0, 2 * (d // tn) + j)),
        ],
        out_specs=pl.BlockSpec((tm, tn), lambda i, j: (i, j)),
        out_shape=jax.ShapeDtypeStruct((t, d), F32),
        compiler_params=_cparams(("parallel", "arbitrary")),
        name="out_proj",
    )(ya, yb, w_out, x, mods)


def _mlp1_kernel(h_ref, w_ref, *refs):
    n_cast = (len(refs) - 1) // 2
    o_ref = refs[n_cast]
    for src, dst in zip(refs[:n_cast], refs[n_cast + 1:]):
        dst[...] = src[...].astype(dst.dtype)
    a = jnp.maximum(jnp.dot(h_ref[...], w_ref[...], preferred_element_type=F32), 0.0)
    o_ref[...] = (a * a).astype(o_ref.dtype)


def _mlp1(h, w1, next_weights, next_layer, tm=1024, tn=1024):
    t, d = h.shape
    n = w1.shape[-1]
    steps = (t // tm) * (n // tn)
    cast_in, cast_out, cast_shapes = [], [], []
    for w in next_weights:
        rows, cols = w.shape[1:]
        slab = rows // steps
        assert rows % steps == 0 and slab % NORM_ROWS == 0
        cast_in.append(pl.BlockSpec((None, slab, cols), lambda i, j: (next_layer, i * (n // tn) + j, 0)))
        cast_out.append(pl.BlockSpec((slab, cols), lambda i, j: (i * (n // tn) + j, 0)))
        cast_shapes.append(jax.ShapeDtypeStruct((rows, cols), BF16))
    outs = pl.pallas_call(
        _mlp1_kernel,
        grid=(t // tm, n // tn),
        in_specs=[
            pl.BlockSpec((tm, d), lambda i, j: (i, 0)),
            pl.BlockSpec((d, tn), lambda i, j: (0, j)),
        ] + cast_in,
        out_specs=[pl.BlockSpec((tm, tn), lambda i, j: (i, j))] + cast_out,
        out_shape=[jax.ShapeDtypeStruct((t, n), BF16)] + cast_shapes,
        compiler_params=_cparams(("arbitrary", "arbitrary")),
        name="mlp_up",
    )(h, w1, *next_weights)
    return outs[0], outs[1:]


def _mlp2_kernel(a_ref, w_ref, x_ref, gt_ref, o_ref, *, blocks_per_batch):
    kk = pl.program_id(2)

    @pl.when(kk == 0)
    def _():
        o_ref[...] = jnp.zeros_like(o_ref)

    o_ref[...] += jnp.dot(a_ref[...], w_ref[...], preferred_element_type=F32)

    @pl.when(kk == pl.num_programs(2) - 1)
    def _():
        b = pl.program_id(0) // blocks_per_batch
        o_ref[...] = x_ref[...] + gt_ref[pl.ds(b, 1), :] * o_ref[...]


def _mlp2(a, w2, x, mods, layer, seq, tm=1024, tn=1024, tk=4096):
    t, d = x.shape
    kdim = a.shape[1]
    return pl.pallas_call(
        functools.partial(_mlp2_kernel, blocks_per_batch=seq // tm),
        grid=(t // tm, d // tn, kdim // tk),
        in_specs=[
            pl.BlockSpec((tm, tk), lambda i, j, k: (i, k)),
            pl.BlockSpec((tk, tn), lambda i, j, k: (k, j)),
            pl.BlockSpec((tm, tn), lambda i, j, k: (i, j)),
            pl.BlockSpec((None, 8, tn), lambda i, j, k: (layer, 0, 5 * (d // tn) + j)),
        ],
        out_specs=pl.BlockSpec((tm, tn), lambda i, j, k: (i, j)),
        out_shape=jax.ShapeDtypeStruct((t, d), F32),
        compiler_params=_cparams(("parallel", "parallel", "arbitrary")),
        name="mlp_down",
    )(a, w2, x, mods)


def _final_norm_kernel(x_ref, g_ref, o_ref):
    rows = 32
    g = g_ref[...]

    def body(r, carry):
        sl = pl.ds(pl.multiple_of(r * rows, rows), rows)
        xf = x_ref[sl, :]
        ms = jnp.mean(xf * xf, axis=-1, keepdims=True)
        o_ref[sl, :] = xf * lax.rsqrt(ms + NORM_EPS) * g
        return carry

    lax.fori_loop(0, x_ref.shape[0] // rows, body, 0)


def _final_norm(x, g, tm=512):
    t, d = x.shape
    return pl.pallas_call(
        _final_norm_kernel,
        grid=(t // tm,),
        in_specs=[pl.BlockSpec((tm, d), lambda i: (i, 0)), pl.BlockSpec((1, d), lambda i: (0, 0))],
        out_specs=pl.BlockSpec((tm, d), lambda i: (i, 0)),
        out_shape=jax.ShapeDtypeStruct((t, d), F32),
        compiler_params=_cparams(("parallel",)),
        name="final_norm",
    )(x, g)


def kernel(x, c, w_ada, b_ada, mod_table, norm1_g, w_in, conv_w, conv_b, conv_ln_g, conv_ln_b, w_alpha,
           b_alpha, gla_norm_g, w_out, norm2_g, w_mlp1, w_mlp2, final_g):
    batch, seq, d = x.shape
    depth = mod_table.shape[0]
    conv_ch = conv_w.shape[-1]
    dk = w_alpha.shape[-1] // GLA_HEADS
    dv = gla_norm_g.shape[-1]
    n_main = w_in.shape[-1] - GLA_LOWRANK
    assert batch <= 8 and mod_table.shape[1] == N_MOD and conv_w.shape[1] == CONV_WIDTH
    assert n_main == 2 * conv_ch + 2 * GLA_HEADS * (dk + dv) and w_alpha.shape[1] == GLA_LOWRANK

    wa = jnp.pad(w_alpha, ((0, 0), (0, LANES - GLA_LOWRANK), (0, 0))).astype(BF16)
    cw = jnp.pad(conv_w, ((0, 0), (0, CONV_HALO - CONV_WIDTH), (0, 0)))
    c8 = jnp.pad(c.astype(F32), ((0, 8 - batch), (0, 0)))
    stacked = (w_in, w_out, w_mlp1, w_mlp2)
    weights = [w[0].astype(BF16) for w in stacked]

    mods = _ada(c8, w_ada, b_ada.reshape(1, -1), mod_table.reshape(depth, 1, N_MOD * d))

    xf = x.reshape(batch * seq, d)
    for l in range(depth):
        w_main, w_out_b, w1, w2 = weights
        w_lr = jnp.pad(w_main[:, n_main:], ((0, 0), (0, LANES - GLA_LOWRANK)))
        h = _norm(xf, norm1_g.reshape(depth, 1, d), mods, l, seq, 0)
        z, alr = _inproj(h, w_main, w_lr, n_main)
        y_a = _conv(z, cw, conv_b.reshape(depth, 1, conv_ch), conv_ln_g.reshape(depth, 1, conv_ch),
                    conv_ln_b.reshape(depth, 1, conv_ch), l, seq, conv_ch)
        y_b = _gla(z, alr, wa, b_alpha.reshape(depth, 1, -1), gla_norm_g.reshape(depth, 1, dv),
                   l, batch, seq, conv_ch, dk, dv)
        xf = _outproj(y_a, y_b, w_out_b, xf, mods, l, seq)
        h = _norm(xf, norm2_g.reshape(depth, 1, d), mods, l, seq, 3)
        hid, weights = _mlp1(h, w1, stacked if l + 1 < depth else (), l + 1)
        xf = _mlp2(hid, w2, xf, mods, l, seq)
    out = _final_norm(xf, final_g.reshape(1, d))
    return out.reshape(batch, seq, d)
```

```python
import functools

import jax
import jax.numpy as jnp
from jax import lax
from jax.experimental import pallas as pl
from jax.experimental.pallas import tpu as pltpu

F32 = jnp.float32
BF16 = jnp.bfloat16

CHUNK = 64
CONV_WIDTH = 31
CONV_HALO = 32
GLA_HEADS = 4
GLA_LOWRANK = 16
GLA_TAU = 16.0
N_MOD = 6
NORM_EPS = 1e-6
LANES = 128
SUBLANES = 8
BF16_ROWS = 16
NORM_ROWS = 32
NORM_COL_PARTS = 4
LN_ROWS = 32
VMEM_LIMIT = 56 * 1024 * 1024


def _cparams(sem):
    return pltpu.CompilerParams(dimension_semantics=sem, vmem_limit_bytes=VMEM_LIMIT)


def _sigmoid(x):
    return 1.0 / (1.0 + jnp.exp(-x))


def _silu(x):
    return x * _sigmoid(x)


def _ada_kernel(c_ref, w_ref, b_ref, t_ref, o_ref):
    s = _silu(c_ref[...]).astype(BF16)
    r = jnp.dot(s, w_ref[...].astype(BF16), preferred_element_type=F32) + b_ref[...]
    o_ref[...] = r[None, :, :] + t_ref[...]


def _ada(c8, w_ada, b_ada, table, tn=1024):
    d, n = w_ada.shape
    depth = table.shape[0]
    return pl.pallas_call(
        _ada_kernel,
        grid=(n // tn,),
        in_specs=[
            pl.BlockSpec((8, d), lambda j: (0, 0)),
            pl.BlockSpec((d, tn), lambda j: (0, j)),
            pl.BlockSpec((1, tn), lambda j: (0, j)),
            pl.BlockSpec((depth, 1, tn), lambda j: (0, 0, j)),
        ],
        out_specs=pl.BlockSpec((depth, 8, tn), lambda j: (0, 0, j)),
        out_shape=jax.ShapeDtypeStruct((depth, 8, n), F32),
        compiler_params=_cparams(("arbitrary",)),
        name="ada_mod",
    )(c8, w_ada, b_ada, table)


def _norm_kernel(x_ref, g_ref, sh_ref, sc_ref, h_ref, p_ref, *, blocks_per_batch):
    b = pl.program_id(0) // blocks_per_batch
    tm, d = x_ref.shape
    sub = p_ref.shape[1]
    groups = NORM_ROWS // sub
    n = tm // NORM_ROWS
    p_ref[0] = jnp.broadcast_to(g_ref[...], (sub, d))
    p_ref[1] = jnp.broadcast_to(1.0 + sc_ref[pl.ds(b, 1), :], (sub, d))
    p_ref[2] = jnp.broadcast_to(sh_ref[pl.ds(b, 1), :], (sub, d))

    def rows_of(r):
        return pl.ds(pl.multiple_of(r * NORM_ROWS, NORM_ROWS), NORM_ROWS)

    dq = d // NORM_COL_PARTS

    def sq_part(r, q):
        xf = x_ref[rows_of(r), q * dq:(q + 1) * dq].reshape(groups, sub, dq)
        return jnp.sum(xf * xf, axis=-1, keepdims=True)

    def sum_sq(r):
        return sum(sq_part(r, q) for q in range(NORM_COL_PARTS))

    def body(r, ss):
        r_next = jnp.minimum(r + 1, n - 1)
        inv = lax.rsqrt(ss * (1.0 / d) + NORM_EPS)
        ss_next = None
        for q in range(NORM_COL_PARTS):
            cs = slice(q * dq, (q + 1) * dq)
            part = sq_part(r_next, q)
            ss_next = part if ss_next is None else ss_next + part
            xf = x_ref[rows_of(r), cs].reshape(groups, sub, dq)
            y = xf * inv * p_ref[0, :, cs][None]
            h = y * p_ref[1, :, cs][None] + p_ref[2, :, cs][None]
            h_ref[rows_of(r), cs] = h.reshape(NORM_ROWS, dq).astype(BF16)
        return ss_next

    lax.fori_loop(0, n, body, sum_sq(0))


def _norm(x, norm_g, mods, layer, seq, shift_seg, tm=512):
    t, d = x.shape
    return pl.pallas_call(
        functools.partial(_norm_kernel, blocks_per_batch=seq // tm),
        grid=(t // tm,),
        in_specs=[
            pl.BlockSpec((tm, d), lambda i: (i, 0)),
            pl.BlockSpec((None, 1, d), lambda i: (layer, 0, 0)),
            pl.BlockSpec((None, 8, d), lambda i: (layer, 0, shift_seg)),
            pl.BlockSpec((None, 8, d), lambda i: (layer, 0, shift_seg + 1)),
        ],
        out_specs=pl.BlockSpec((tm, d), lambda i: (i, 0)),
        out_shape=jax.ShapeDtypeStruct((t, d), BF16),
        scratch_shapes=[pltpu.VMEM((3, SUBLANES, d), F32)],
        compiler_params=_cparams(("parallel",)),
        name="norm_mod",
    )(x, norm_g, mods, mods)


def _inproj_kernel(h_ref, w_ref, wlr_ref, z_ref, alr_ref):
    @pl.when(pl.program_id(1) == 0)
    def _():
        alr_ref[...] = jnp.dot(h_ref[...], wlr_ref[...], preferred_element_type=F32)

    z_ref[...] = jnp.dot(h_ref[...], w_ref[...], preferred_element_type=F32).astype(z_ref.dtype)


def _inproj(h, w_main, w_lr, n, layer, tm=1024, tn=1024):
    t, d = h.shape
    assert n % tn == 0 and n <= w_main.shape[-1]
    return pl.pallas_call(
        _inproj_kernel,
        grid=(t // tm, n // tn),
        in_specs=[
            pl.BlockSpec((tm, d), lambda i, j: (i, 0)),
            pl.BlockSpec((None, d, tn), lambda i, j: (layer, 0, j)),
            pl.BlockSpec((None, d, LANES), lambda i, j: (layer, 0, 0)),
        ],
        out_specs=[
            pl.BlockSpec((tm, tn), lambda i, j: (i, j)),
            pl.BlockSpec((tm, LANES), lambda i, j: (i, 0)),
        ],
        out_shape=[
            jax.ShapeDtypeStruct((t, n), F32),
            jax.ShapeDtypeStruct((t, LANES), F32),
        ],
        compiler_params=_cparams(("parallel", "arbitrary")),
        name="in_proj",
    )(h, w_main, w_lr)


def _conv_kernel(av_ref, ag_ref, hv_ref, hg_ref, cw_ref, cb_ref, lg_ref, lb_ref, o_ref, u_ref, y_ref, p_ref,
                 *, blocks_per_batch, lane_tile):
    n = pl.program_id(0)
    l = pl.program_id(1)
    ts = av_ref.shape[0]
    first = (n % blocks_per_batch) == 0

    hu = hv_ref[...] * _sigmoid(hg_ref[...])
    u_ref[0:CONV_HALO, :] = jnp.where(first, 0.0, hu)
    u_ref[CONV_HALO:, :] = av_ref[...] * _sigmoid(ag_ref[...])

    off = CONV_HALO - (CONV_WIDTH - 1)
    slab_rows = CHUNK + CONV_HALO
    for tt in range(ts // CHUNK):
        for lt in range(lane_tile // LANES):
            ls = slice(lt * LANES, (lt + 1) * LANES)
            slab = u_ref[tt * CHUNK:tt * CHUNK + slab_rows, ls]
            acc = jnp.broadcast_to(cb_ref[:, ls], (CHUNK, LANES))
            for res in range(8):
                taps = [w for w in range(CONV_WIDTH) if (off + w) % 8 == res]
                rot = slab if res == 0 else pltpu.roll(slab, slab_rows - res, axis=0)
                for w in taps:
                    a0 = (off + w) // 8 * 8
                    acc = acc + rot[a0:a0 + CHUNK, :] * cw_ref[w:w + 1, ls]
            y_ref[l, tt * CHUNK:(tt + 1) * CHUNK, ls] = acc

    @pl.when(l == pl.num_programs(1) - 1)
    def _():
        nl = y_ref.shape[0]
        c = nl * lane_tile
        rows = p_ref.shape[1]
        n = ts // rows

        def rows_of(r):
            return pl.ds(pl.multiple_of(jnp.minimum(r, n - 1) * rows, rows), rows)

        def mean_of(r):
            tot = y_ref[0, rows_of(r), :]
            for k in range(1, nl):
                tot = tot + y_ref[k, rows_of(r), :]
            return jnp.sum(tot, axis=-1, keepdims=True) / c

        def inv_std_of(r, mu):
            tot = jnp.square(y_ref[0, rows_of(r), :] - mu)
            for k in range(1, nl):
                tot = tot + jnp.square(y_ref[k, rows_of(r), :] - mu)
            return lax.rsqrt(jnp.sum(tot, axis=-1, keepdims=True) / c + NORM_EPS)

        def body(r, carry):
            mu0, inv0, mu1 = carry
            mu2 = mean_of(r + 2)
            inv1 = inv_std_of(r + 1, mu1)
            for k in range(nl):
                cs = slice(k * lane_tile, (k + 1) * lane_tile)
                yn = (y_ref[k, rows_of(r), :] - mu0) * inv0 * p_ref[0, :, cs] + p_ref[1, :, cs]
                o_ref[rows_of(r), cs] = _silu(yn).astype(o_ref.dtype)
            return mu1, inv1, mu2

        p_ref[0] = jnp.broadcast_to(lg_ref[...], (rows, c))
        p_ref[1] = jnp.broadcast_to(lb_ref[...], (rows, c))
        mu_first = mean_of(0)
        lax.fori_loop(0, n, body, (mu_first, inv_std_of(0, mu_first), mean_of(1)))


def _conv(z, conv_w, conv_b, ln_g, ln_b, layer, seq, conv_ch, ts=512, lane_tile=512):
    t = z.shape[0]
    nl = conv_ch // lane_tile
    hb = ts // CONV_HALO

    def halo_idx(col0):
        return lambda n, l: (jnp.maximum(n * hb - 1, 0), col0 + l)

    return pl.pallas_call(
        functools.partial(_conv_kernel, blocks_per_batch=seq // ts, lane_tile=lane_tile),
        grid=(t // ts, nl),
        in_specs=[
            pl.BlockSpec((ts, lane_tile), lambda n, l: (n, l)),
            pl.BlockSpec((ts, lane_tile), lambda n, l: (n, nl + l)),
            pl.BlockSpec((CONV_HALO, lane_tile), halo_idx(0)),
            pl.BlockSpec((CONV_HALO, lane_tile), halo_idx(nl)),
            pl.BlockSpec((None, CONV_HALO, lane_tile), lambda n, l: (layer, 0, l)),
            pl.BlockSpec((None, 1, lane_tile), lambda n, l: (layer, 0, l)),
            pl.BlockSpec((None, 1, conv_ch), lambda n, l: (layer, 0, 0)),
            pl.BlockSpec((None, 1, conv_ch), lambda n, l: (layer, 0, 0)),
        ],
        out_specs=pl.BlockSpec((ts, conv_ch), lambda n, l: (n, 0)),
        out_shape=jax.ShapeDtypeStruct((t, conv_ch), BF16),
        scratch_shapes=[
            pltpu.VMEM((CONV_HALO + ts, lane_tile), F32),
            pltpu.VMEM((nl, ts, lane_tile), F32),
            pltpu.VMEM((2, LN_ROWS, conv_ch), F32),
        ],
        compiler_params=_cparams(("parallel", "arbitrary")),
        name="conformer_conv",
    )(z, z, z, z, conv_w, conv_b, ln_g, ln_b)


def _gla_kernel(q_ref, k_ref, v_ref, g_ref, alr_ref, wa_ref, ba_ref, ng_ref, o_ref, st_ref, *, q_scale):
    @pl.when(pl.program_id(2) == 0)
    def _():
        st_ref[...] = jnp.zeros_like(st_ref)

    heads, dv, dk = st_ref.shape
    rows = q_ref.shape[0]
    x = jnp.dot(alr_ref[...].astype(BF16), wa_ref[...], preferred_element_type=F32) + ba_ref[...]
    log_a = -(jnp.maximum(-x, 0.0) + jnp.log(1.0 + jnp.exp(-jnp.abs(x)))) * (1.0 / GLA_TAU)
    sub = lax.broadcasted_iota(jnp.int32, (rows, heads * dk), 0) & 7
    c = log_a
    for s in (1, 2, 4):
        c = c + jnp.where(sub >= s, pltpu.roll(c, s, axis=0), 0.0)

    tril = (lax.broadcasted_iota(jnp.int32, (CHUNK, CHUNK), 0)
            >= lax.broadcasted_iota(jnp.int32, (CHUNK, CHUNK), 1))
    contract_last = (((1,), (1,)), ((), ()))
    contract_first = (((0,), (0,)), ((), ()))

    for ci in range(rows // CHUNK):
        base = ci * CHUNK
        parts, carry = [], None
        for j in range(CHUNK // 8):
            blk = c[base + 8 * j:base + 8 * j + 8, :]
            if carry is not None:
                blk = blk + carry
            parts.append(blk)
            carry = blk[7:8, :]
        cum_all = jnp.concatenate(parts, axis=0)
        rs = slice(base, base + CHUNK)
        for hh in range(heads):
            ks = slice(hh * dk, (hh + 1) * dk)
            vs = slice(hh * dv, (hh + 1) * dv)
            cum = cum_all[:, ks]
            cl = carry[:, ks]
            kc = k_ref[rs, ks]
            q_dec = ((q_ref[rs, ks] * q_scale) * jnp.exp(cum)).astype(BF16)
            k_in = (kc * jnp.exp(-cum)).astype(BF16)
            k_st = (kc * jnp.exp(cl - cum)).astype(BF16)
            vb = v_ref[rs, vs].astype(BF16)

            scores = lax.dot_general(q_dec, k_in, contract_last, preferred_element_type=F32)
            scores = jnp.where(tril, scores, 0.0).astype(BF16)
            st = st_ref[hh]
            o = (jnp.dot(scores, vb, preferred_element_type=F32)
                 + lax.dot_general(q_dec, st.astype(BF16), contract_last, preferred_element_type=F32))
            upd = lax.dot_general(vb, k_st, contract_first, preferred_element_type=F32)
            st_ref[hh] = st * jnp.exp(cl) + upd

            ms = jnp.mean(o * o, axis=-1, keepdims=True)
            on = o * lax.rsqrt(ms + NORM_EPS) * ng_ref[...]
            o_ref[rs, vs] = (on * _silu(g_ref[rs, vs])).astype(o_ref.dtype)


def _gla(z, alr, w_alpha, b_alpha, norm_g, layer, batch, seq, conv_ch, dk, dv, rows=512, heads=2):
    t = z.shape[0]
    nb = seq // rows
    hg = GLA_HEADS // heads
    wk, wv = heads * dk, heads * dv
    q_col = 2 * conv_ch // wk
    k_col = q_col + hg
    v_col = (2 * conv_ch + 2 * GLA_HEADS * dk) // wv
    g_col = v_col + hg
    return pl.pallas_call(
        functools.partial(_gla_kernel, q_scale=float(dk) ** -0.5),
        grid=(batch, hg, nb),
        in_specs=[
            pl.BlockSpec((rows, wk), lambda b, h, n: (b * nb + n, q_col + h)),
            pl.BlockSpec((rows, wk), lambda b, h, n: (b * nb + n, k_col + h)),
            pl.BlockSpec((rows, wv), lambda b, h, n: (b * nb + n, v_col + h)),
            pl.BlockSpec((rows, wv), lambda b, h, n: (b * nb + n, g_col + h)),
            pl.BlockSpec((rows, LANES), lambda b, h, n: (b * nb + n, 0)),
            pl.BlockSpec((None, LANES, wk), lambda b, h, n: (layer, 0, h)),
            pl.BlockSpec((None, 1, wk), lambda b, h, n: (layer, 0, h)),
            pl.BlockSpec((None, 1, dv), lambda b, h, n: (layer, 0, 0)),
        ],
        out_specs=pl.BlockSpec((rows, wv), lambda b, h, n: (b * nb + n, h)),
        out_shape=jax.ShapeDtypeStruct((t, GLA_HEADS * dv), BF16),
        scratch_shapes=[pltpu.VMEM((heads, dv, dk), F32)],
        compiler_params=_cparams(("parallel", "parallel", "arbitrary")),
        name="gla",
    )(z, z, z, z, alr, w_alpha, b_alpha, norm_g)


def _outproj_kernel(ya_ref, yb_ref, w_ref, x_ref, gt_ref, o_ref, *, blocks_per_batch):
    ca = ya_ref.shape[1]
    y = (jnp.dot(ya_ref[...], w_ref[0:ca, :], preferred_element_type=F32)
         + jnp.dot(yb_ref[...], w_ref[ca:, :], preferred_element_type=F32))
    b = pl.program_id(0) // blocks_per_batch
    o_ref[...] = x_ref[...] + gt_ref[pl.ds(b, 1), :] * y


def _outproj(ya, yb, w_out, x, mods, layer, seq, tm=1024, tn=1024):
    t, d = x.shape
    ca, cb = ya.shape[1], yb.shape[1]
    return pl.pallas_call(
        functools.partial(_outproj_kernel, blocks_per_batch=seq // tm),
        grid=(t // tm, d // tn),
        in_specs=[
            pl.BlockSpec((tm, ca), lambda i, j: (i, 0)),
            pl.BlockSpec((tm, cb), lambda i, j: (i, 0)),
            pl.BlockSpec((ca + cb, tn), lambda i, j: (0, j)),
            pl.BlockSpec((tm, tn), lambda i, j: (i, j)),
            pl.BlockSpec((None, 8, tn), lambda i, j: (layer, 0, 2 * (d // tn) + j)),
        ],
        out_specs=pl.BlockSpec((tm, tn), lambda i, j: (i, j)),
        out_shape=jax.ShapeDtypeStruct((t, d), F32),
        compiler_params=_cparams(("parallel", "arbitrary")),
        name="out_proj",
    )(ya, yb, w_out, x, mods)


def _mlp1_kernel(h_ref, w_ref, *refs):
    n_cast = (len(refs) - 1) // 2
    o_ref = refs[n_cast]
    for src, dst in zip(refs[:n_cast], refs[n_cast + 1:]):
        dst[...] = src[...].astype(dst.dtype)
    a = jnp.maximum(jnp.dot(h_ref[...], w_ref[...], preferred_element_type=F32), 0.0)
    o_ref[...] = (a * a).astype(o_ref.dtype)


def _mlp1(h, w1, next_weights, next_layer, tm=1024, tn=1024):
    t, d = h.shape
    n = w1.shape[-1]
    steps = (t // tm) * (n // tn)
    cast_in, cast_out, cast_shapes = [], [], []
    for w in next_weights:
        rows, cols = w.shape[1:]
        slab = rows // steps
        assert rows % steps == 0 and slab % BF16_ROWS == 0
        cast_in.append(pl.BlockSpec((None, slab, cols), lambda i, j: (next_layer, i * (n // tn) + j, 0)))
        cast_out.append(pl.BlockSpec((slab, cols), lambda i, j: (i * (n // tn) + j, 0)))
        cast_shapes.append(jax.ShapeDtypeStruct((rows, cols), BF16))
    outs = pl.pallas_call(
        _mlp1_kernel,
        grid=(t // tm, n // tn),
        in_specs=[
            pl.BlockSpec((tm, d), lambda i, j: (i, 0)),
            pl.BlockSpec((d, tn), lambda i, j: (0, j)),
        ] + cast_in,
        out_specs=[pl.BlockSpec((tm, tn), lambda i, j: (i, j))] + cast_out,
        out_shape=[jax.ShapeDtypeStruct((t, n), BF16)] + cast_shapes,
        compiler_params=_cparams(("arbitrary", "arbitrary")),
        name="mlp_up",
    )(h, w1, *next_weights)
    return outs[0], outs[1:]


def _mlp2_kernel(a_ref, w_ref, x_ref, gt_ref, o_ref, *, blocks_per_batch):
    kk = pl.program_id(2)

    @pl.when(kk == 0)
    def _():
        o_ref[...] = jnp.zeros_like(o_ref)

    o_ref[...] += jnp.dot(a_ref[...], w_ref[...], preferred_element_type=F32)

    @pl.when(kk == pl.num_programs(2) - 1)
    def _():
        b = pl.program_id(0) // blocks_per_batch
        o_ref[...] = x_ref[...] + gt_ref[pl.ds(b, 1), :] * o_ref[...]


def _mlp2(a, w2, x, mods, layer, seq, tm=1024, tn=1024, tk=4096):
    t, d = x.shape
    kdim = a.shape[1]
    return pl.pallas_call(
        functools.partial(_mlp2_kernel, blocks_per_batch=seq // tm),
        grid=(t // tm, d // tn, kdim // tk),
        in_specs=[
            pl.BlockSpec((tm, tk), lambda i, j, k: (i, k)),
            pl.BlockSpec((tk, tn), lambda i, j, k: (k, j)),
            pl.BlockSpec((tm, tn), lambda i, j, k: (i, j)),
            pl.BlockSpec((None, 8, tn), lambda i, j, k: (layer, 0, 5 * (d // tn) + j)),
        ],
        out_specs=pl.BlockSpec((tm, tn), lambda i, j, k: (i, j)),
        out_shape=jax.ShapeDtypeStruct((t, d), F32),
        compiler_params=_cparams(("parallel", "parallel", "arbitrary")),
        name="mlp_down",
    )(a, w2, x, mods)


def _final_norm_kernel(x_ref, g_ref, o_ref):
    rows = 32
    g = g_ref[...]

    def body(r, carry):
        sl = pl.ds(pl.multiple_of(r * rows, rows), rows)
        xf = x_ref[sl, :]
        ms = jnp.mean(xf * xf, axis=-1, keepdims=True)
        o_ref[sl, :] = xf * lax.rsqrt(ms + NORM_EPS) * g
        return carry

    lax.fori_loop(0, x_ref.shape[0] // rows, body, 0)


def _final_norm(x, g, tm=512):
    t, d = x.shape
    return pl.pallas_call(
        _final_norm_kernel,
        grid=(t // tm,),
        in_specs=[pl.BlockSpec((tm, d), lambda i: (i, 0)), pl.BlockSpec((1, d), lambda i: (0, 0))],
        out_specs=pl.BlockSpec((tm, d), lambda i: (i, 0)),
        out_shape=jax.ShapeDtypeStruct((t, d), F32),
        compiler_params=_cparams(("parallel",)),
        name="final_norm",
    )(x, g)


def kernel(x, c, w_ada, b_ada, mod_table, norm1_g, w_in, conv_w, conv_b, conv_ln_g, conv_ln_b, w_alpha,
           b_alpha, gla_norm_g, w_out, norm2_g, w_mlp1, w_mlp2, final_g):
    batch, seq, d = x.shape
    depth = mod_table.shape[0]
    conv_ch = conv_w.shape[-1]
    dk = w_alpha.shape[-1] // GLA_HEADS
    dv = gla_norm_g.shape[-1]
    n_main = w_in.shape[-1] - GLA_LOWRANK
    assert batch <= 8 and mod_table.shape[1] == N_MOD and conv_w.shape[1] == CONV_WIDTH
    assert n_main == 2 * conv_ch + 2 * GLA_HEADS * (dk + dv) and w_alpha.shape[1] == GLA_LOWRANK

    w_main = w_in.astype(BF16)
    w_lr = jnp.pad(w_main[:, :, n_main:], ((0, 0), (0, 0), (0, LANES - GLA_LOWRANK)))
    wa = jnp.pad(w_alpha, ((0, 0), (0, LANES - GLA_LOWRANK), (0, 0))).astype(BF16)
    cw = jnp.pad(conv_w, ((0, 0), (0, CONV_HALO - CONV_WIDTH), (0, 0)))
    c8 = jnp.pad(c.astype(F32), ((0, 8 - batch), (0, 0)))
    stacked = (w_out, w_mlp1, w_mlp2)
    weights = [w[0].astype(BF16) for w in stacked]

    mods = _ada(c8, w_ada, b_ada.reshape(1, -1), mod_table.reshape(depth, 1, N_MOD * d))

    xf = x.reshape(batch * seq, d)
    for l in range(depth):
        w_out_b, w1, w2 = weights
        h = _norm(xf, norm1_g.reshape(depth, 1, d), mods, l, seq, 0)
        z, alr = _inproj(h, w_main, w_lr, n_main, l)
        y_a = _conv(z, cw, conv_b.reshape(depth, 1, conv_ch), conv_ln_g.reshape(depth, 1, conv_ch),
                    conv_ln_b.reshape(depth, 1, conv_ch), l, seq, conv_ch)
        y_b = _gla(z, alr, wa, b_alpha.reshape(depth, 1, -1), gla_norm_g.reshape(depth, 1, dv),
                   l, batch, seq, conv_ch, dk, dv)
        xf = _outproj(y_a, y_b, w_out_b, xf, mods, l, seq)
        h = _norm(xf, norm2_g.reshape(depth, 1, d), mods, l, seq, 3)
        hid, weights = _mlp1(h, w1, stacked if l + 1 < depth else (), l + 1)
        xf = _mlp2(hid, w2, xf, mods, l, seq)
    out = _final_norm(xf, final_g.reshape(1, d))
    return out.reshape(batch, seq, d)
```

```python
import functools

import jax
import jax.numpy as jnp
from jax import lax
from jax.experimental import pallas as pl
from jax.experimental.pallas import tpu as pltpu

F32 = jnp.float32
BF16 = jnp.bfloat16

CHUNK = 64
CONV_WIDTH = 31
CONV_HALO = 32
GLA_HEADS = 4
GLA_LOWRANK = 16
GLA_TAU = 16.0
N_MOD = 6
NORM_EPS = 1e-6
LANES = 128
SUBLANES = 8
BF16_ROWS = 16
NORM_ROWS = 32
NORM_COL_PARTS = 4
LN_ROWS = 32
VMEM_LIMIT = 56 * 1024 * 1024


def _cparams(sem):
    return pltpu.CompilerParams(dimension_semantics=sem, vmem_limit_bytes=VMEM_LIMIT)


def _sigmoid(x):
    return 1.0 / (1.0 + jnp.exp(-x))


def _silu(x):
    return x * _sigmoid(x)


def _ada_kernel(c_ref, w_ref, b_ref, t_ref, o_ref):
    s = _silu(c_ref[...]).astype(BF16)
    r = jnp.dot(s, w_ref[...].astype(BF16), preferred_element_type=F32) + b_ref[...]
    o_ref[...] = r[None, :, :] + t_ref[...]


def _ada(c8, w_ada, b_ada, table, tn=1024):
    d, n = w_ada.shape
    depth = table.shape[0]
    return pl.pallas_call(
        _ada_kernel,
        grid=(n // tn,),
        in_specs=[
            pl.BlockSpec((8, d), lambda j: (0, 0)),
            pl.BlockSpec((d, tn), lambda j: (0, j)),
            pl.BlockSpec((1, tn), lambda j: (0, j)),
            pl.BlockSpec((depth, 1, tn), lambda j: (0, 0, j)),
        ],
        out_specs=pl.BlockSpec((depth, 8, tn), lambda j: (0, 0, j)),
        out_shape=jax.ShapeDtypeStruct((depth, 8, n), F32),
        compiler_params=_cparams(("arbitrary",)),
        name="ada_mod",
    )(c8, w_ada, b_ada, table)


def _norm_kernel(x_ref, g_ref, sh_ref, sc_ref, h_ref, p_ref, *, blocks_per_batch):
    b = pl.program_id(0) // blocks_per_batch
    tm, d = x_ref.shape
    sub = p_ref.shape[1]
    groups = NORM_ROWS // sub
    n = tm // NORM_ROWS
    p_ref[0] = jnp.broadcast_to(g_ref[...], (sub, d))
    p_ref[1] = jnp.broadcast_to(1.0 + sc_ref[pl.ds(b, 1), :], (sub, d))
    p_ref[2] = jnp.broadcast_to(sh_ref[pl.ds(b, 1), :], (sub, d))

    def rows_of(r):
        return pl.ds(pl.multiple_of(r * NORM_ROWS, NORM_ROWS), NORM_ROWS)

    dq = d // NORM_COL_PARTS

    def sq_part(r, q):
        xf = x_ref[rows_of(r), q * dq:(q + 1) * dq].reshape(groups, sub, dq)
        return jnp.sum(xf * xf, axis=-1, keepdims=True)

    def sum_sq(r):
        return sum(sq_part(r, q) for q in range(NORM_COL_PARTS))

    def body(r, ss):
        r_next = jnp.minimum(r + 1, n - 1)
        inv = lax.rsqrt(ss * (1.0 / d) + NORM_EPS)
        ss_next = None
        for q in range(NORM_COL_PARTS):
            cs = slice(q * dq, (q + 1) * dq)
            part = sq_part(r_next, q)
            ss_next = part if ss_next is None else ss_next + part
            xf = x_ref[rows_of(r), cs].reshape(groups, sub, dq)
            y = xf * inv * p_ref[0, :, cs][None]
            h = y * p_ref[1, :, cs][None] + p_ref[2, :, cs][None]
            h_ref[rows_of(r), cs] = h.reshape(NORM_ROWS, dq).astype(BF16)
        return ss_next

    lax.fori_loop(0, n, body, sum_sq(0))


def _norm(x, norm_g, mods, layer, seq, shift_seg, tm=512):
    t, d = x.shape
    return pl.pallas_call(
        functools.partial(_norm_kernel, blocks_per_batch=seq // tm),
        grid=(t // tm,),
        in_specs=[
            pl.BlockSpec((tm, d), lambda i: (i, 0)),
            pl.BlockSpec((None, 1, d), lambda i: (layer, 0, 0)),
            pl.BlockSpec((None, 8, d), lambda i: (layer, 0, shift_seg)),
            pl.BlockSpec((None, 8, d), lambda i: (layer, 0, shift_seg + 1)),
        ],
        out_specs=pl.BlockSpec((tm, d), lambda i: (i, 0)),
        out_shape=jax.ShapeDtypeStruct((t, d), BF16),
        scratch_shapes=[pltpu.VMEM((3, SUBLANES, d), F32)],
        compiler_params=_cparams(("parallel",)),
        name="norm_mod",
    )(x, norm_g, mods, mods)


def _inproj_kernel(h_ref, w_ref, wlr_ref, *refs):
    n_cast = (len(refs) - 2) // 2
    z_ref, alr_ref = refs[n_cast], refs[n_cast + 1]
    for src, dst in zip(refs[:n_cast], refs[n_cast + 2:]):
        dst[...] = src[...].astype(dst.dtype)

    @pl.when(pl.program_id(1) == 0)
    def _():
        alr_ref[...] = jnp.dot(h_ref[...], wlr_ref[...], preferred_element_type=F32)

    z_ref[...] = jnp.dot(h_ref[...], w_ref[...], preferred_element_type=F32).astype(z_ref.dtype)


def _inproj(h, w_main, w_lr, n, layer, cast_weights=(), tm=1024, tn=1024):
    t, d = h.shape
    assert n % tn == 0 and n <= w_main.shape[-1]
    nj = n // tn
    slabs = 1 << (((t // tm) * nj).bit_length() - 1)
    cast_in, cast_out, cast_shapes = [], [], []
    for w in cast_weights:
        rows, cols = w.shape[1:]
        slab = rows // slabs
        assert rows % slabs == 0 and slab % BF16_ROWS == 0
        cast_in.append(pl.BlockSpec((None, slab, cols), lambda i, j: (layer, jnp.minimum(i * nj + j, slabs - 1), 0)))
        cast_out.append(pl.BlockSpec((slab, cols), lambda i, j: (jnp.minimum(i * nj + j, slabs - 1), 0)))
        cast_shapes.append(jax.ShapeDtypeStruct((rows, cols), BF16))
    outs = pl.pallas_call(
        _inproj_kernel,
        grid=(t // tm, nj),
        in_specs=[
            pl.BlockSpec((tm, d), lambda i, j: (i, 0)),
            pl.BlockSpec((None, d, tn), lambda i, j: (layer, 0, j)),
            pl.BlockSpec((None, d, LANES), lambda i, j: (layer, 0, 0)),
        ] + cast_in,
        out_specs=[
            pl.BlockSpec((tm, tn), lambda i, j: (i, j)),
            pl.BlockSpec((tm, LANES), lambda i, j: (i, 0)),
        ] + cast_out,
        out_shape=[
            jax.ShapeDtypeStruct((t, n), F32),
            jax.ShapeDtypeStruct((t, LANES), F32),
        ] + cast_shapes,
        compiler_params=_cparams(("arbitrary", "arbitrary")),
        name="in_proj",
    )(h, w_main, w_lr, *cast_weights)
    return outs[0], outs[1], outs[2:]


def _conv_kernel(av_ref, ag_ref, hv_ref, hg_ref, cw_ref, cb_ref, lg_ref, lb_ref, o_ref, u_ref, y_ref, p_ref,
                 *, blocks_per_batch, lane_tile):
    n = pl.program_id(0)
    l = pl.program_id(1)
    ts = av_ref.shape[0]
    first = (n % blocks_per_batch) == 0

    hu = hv_ref[...] * _sigmoid(hg_ref[...])
    u_ref[0:CONV_HALO, :] = jnp.where(first, 0.0, hu)
    u_ref[CONV_HALO:, :] = av_ref[...] * _sigmoid(ag_ref[...])

    off = CONV_HALO - (CONV_WIDTH - 1)
    slab_rows = CHUNK + CONV_HALO
    for tt in range(ts // CHUNK):
        for lt in range(lane_tile // LANES):
            ls = slice(lt * LANES, (lt + 1) * LANES)
            slab = u_ref[tt * CHUNK:tt * CHUNK + slab_rows, ls]
            acc = jnp.broadcast_to(cb_ref[:, ls], (CHUNK, LANES))
            for res in range(8):
                taps = [w for w in range(CONV_WIDTH) if (off + w) % 8 == res]
                rot = slab if res == 0 else pltpu.roll(slab, slab_rows - res, axis=0)
                for w in taps:
                    a0 = (off + w) // 8 * 8
                    acc = acc + rot[a0:a0 + CHUNK, :] * cw_ref[w:w + 1, ls]
            y_ref[l, tt * CHUNK:(tt + 1) * CHUNK, ls] = acc

    @pl.when(l == pl.num_programs(1) - 1)
    def _():
        nl = y_ref.shape[0]
        c = nl * lane_tile
        rows = p_ref.shape[1]
        n = ts // rows

        def rows_of(r):
            return pl.ds(pl.multiple_of(jnp.minimum(r, n - 1) * rows, rows), rows)

        def mean_of(r):
            tot = y_ref[0, rows_of(r), :]
            for k in range(1, nl):
                tot = tot + y_ref[k, rows_of(r), :]
            return jnp.sum(tot, axis=-1, keepdims=True) / c

        def inv_std_of(r, mu):
            tot = jnp.square(y_ref[0, rows_of(r), :] - mu)
            for k in range(1, nl):
                tot = tot + jnp.square(y_ref[k, rows_of(r), :] - mu)
            return lax.rsqrt(jnp.sum(tot, axis=-1, keepdims=True) / c + NORM_EPS)

        def body(r, carry):
            mu0, inv0, mu1 = carry
            mu2 = mean_of(r + 2)
            inv1 = inv_std_of(r + 1, mu1)
            for k in range(nl):
                cs = slice(k * lane_tile, (k + 1) * lane_tile)
                yn = (y_ref[k, rows_of(r), :] - mu0) * inv0 * p_ref[0, :, cs] + p_ref[1, :, cs]
                o_ref[rows_of(r), cs] = _silu(yn).astype(o_ref.dtype)
            return mu1, inv1, mu2

        p_ref[0] = jnp.broadcast_to(lg_ref[...], (rows, c))
        p_ref[1] = jnp.broadcast_to(lb_ref[...], (rows, c))
        mu_first = mean_of(0)
        lax.fori_loop(0, n, body, (mu_first, inv_std_of(0, mu_first), mean_of(1)))


def _conv(z, conv_w, conv_b, ln_g, ln_b, layer, seq, conv_ch, ts=512, lane_tile=512):
    t = z.shape[0]
    nl = conv_ch // lane_tile
    hb = ts // CONV_HALO

    def halo_idx(col0):
        return lambda n, l: (jnp.maximum(n * hb - 1, 0), col0 + l)

    return pl.pallas_call(
        functools.partial(_conv_kernel, blocks_per_batch=seq // ts, lane_tile=lane_tile),
        grid=(t // ts, nl),
        in_specs=[
            pl.BlockSpec((ts, lane_tile), lambda n, l: (n, l)),
            pl.BlockSpec((ts, lane_tile), lambda n, l: (n, nl + l)),
            pl.BlockSpec((CONV_HALO, lane_tile), halo_idx(0)),
            pl.BlockSpec((CONV_HALO, lane_tile), halo_idx(nl)),
            pl.BlockSpec((None, CONV_HALO, lane_tile), lambda n, l: (layer, 0, l)),
            pl.BlockSpec((None, 1, lane_tile), lambda n, l: (layer, 0, l)),
            pl.BlockSpec((None, 1, conv_ch), lambda n, l: (layer, 0, 0)),
            pl.BlockSpec((None, 1, conv_ch), lambda n, l: (layer, 0, 0)),
        ],
        out_specs=pl.BlockSpec((ts, conv_ch), lambda n, l: (n, 0)),
        out_shape=jax.ShapeDtypeStruct((t, conv_ch), BF16),
        scratch_shapes=[
            pltpu.VMEM((CONV_HALO + ts, lane_tile), F32),
            pltpu.VMEM((nl, ts, lane_tile), F32),
            pltpu.VMEM((2, LN_ROWS, conv_ch), F32),
        ],
        compiler_params=_cparams(("parallel", "arbitrary")),
        name="conformer_conv",
    )(z, z, z, z, conv_w, conv_b, ln_g, ln_b)


def _gla_kernel(q_ref, k_ref, v_ref, g_ref, alr_ref, wa_ref, ba_ref, ng_ref, o_ref, st_ref, *, q_scale):
    @pl.when(pl.program_id(2) == 0)
    def _():
        st_ref[...] = jnp.zeros_like(st_ref)

    heads, dv, dk = st_ref.shape
    rows = q_ref.shape[0]
    x = jnp.dot(alr_ref[...].astype(BF16), wa_ref[...], preferred_element_type=F32) + ba_ref[...]
    log_a = -(jnp.maximum(-x, 0.0) + jnp.log(1.0 + jnp.exp(-jnp.abs(x)))) * (1.0 / GLA_TAU)
    sub = lax.broadcasted_iota(jnp.int32, (rows, heads * dk), 0) & 7
    c = log_a
    for s in (1, 2, 4):
        c = c + jnp.where(sub >= s, pltpu.roll(c, s, axis=0), 0.0)

    tril = (lax.broadcasted_iota(jnp.int32, (CHUNK, CHUNK), 0)
            >= lax.broadcasted_iota(jnp.int32, (CHUNK, CHUNK), 1))
    contract_last = (((1,), (1,)), ((), ()))
    contract_first = (((0,), (0,)), ((), ()))

    for ci in range(rows // CHUNK):
        base = ci * CHUNK
        parts, carry = [], None
        for j in range(CHUNK // 8):
            blk = c[base + 8 * j:base + 8 * j + 8, :]
            if carry is not None:
                blk = blk + carry
            parts.append(blk)
            carry = blk[7:8, :]
        cum_all = jnp.concatenate(parts, axis=0)
        rs = slice(base, base + CHUNK)
        for hh in range(heads):
            ks = slice(hh * dk, (hh + 1) * dk)
            vs = slice(hh * dv, (hh + 1) * dv)
            cum = cum_all[:, ks]
            cl = carry[:, ks]
            kc = k_ref[rs, ks]
            q_dec = ((q_ref[rs, ks] * q_scale) * jnp.exp(cum)).astype(BF16)
            k_in = (kc * jnp.exp(-cum)).astype(BF16)
            k_st = (kc * jnp.exp(cl - cum)).astype(BF16)
            vb = v_ref[rs, vs].astype(BF16)

            scores = lax.dot_general(q_dec, k_in, contract_last, preferred_element_type=F32)
            scores = jnp.where(tril, scores, 0.0).astype(BF16)
            st = st_ref[hh]
            o = (jnp.dot(scores, vb, preferred_element_type=F32)
                 + lax.dot_general(q_dec, st.astype(BF16), contract_last, preferred_element_type=F32))
            upd = lax.dot_general(vb, k_st, contract_first, preferred_element_type=F32)
            st_ref[hh] = st * jnp.exp(cl) + upd

            ms = jnp.mean(o * o, axis=-1, keepdims=True)
            on = o * lax.rsqrt(ms + NORM_EPS) * ng_ref[...]
            o_ref[rs, vs] = (on * _silu(g_ref[rs, vs])).astype(o_ref.dtype)


def _gla(z, alr, w_alpha, b_alpha, norm_g, layer, batch, seq, conv_ch, dk, dv, rows=512, heads=2):
    t = z.shape[0]
    nb = seq // rows
    hg = GLA_HEADS // heads
    wk, wv = heads * dk, heads * dv
    q_col = 2 * conv_ch // wk
    k_col = q_col + hg
    v_col = (2 * conv_ch + 2 * GLA_HEADS * dk) // wv
    g_col = v_col + hg
    return pl.pallas_call(
        functools.partial(_gla_kernel, q_scale=float(dk) ** -0.5),
        grid=(batch, hg, nb),
        in_specs=[
            pl.BlockSpec((rows, wk), lambda b, h, n: (b * nb + n, q_col + h)),
            pl.BlockSpec((rows, wk), lambda b, h, n: (b * nb + n, k_col + h)),
            pl.BlockSpec((rows, wv), lambda b, h, n: (b * nb + n, v_col + h)),
            pl.BlockSpec((rows, wv), lambda b, h, n: (b * nb + n, g_col + h)),
            pl.BlockSpec((rows, LANES), lambda b, h, n: (b * nb + n, 0)),
            pl.BlockSpec((None, LANES, wk), lambda b, h, n: (layer, 0, h)),
            pl.BlockSpec((None, 1, wk), lambda b, h, n: (layer, 0, h)),
            pl.BlockSpec((None, 1, dv), lambda b, h, n: (layer, 0, 0)),
        ],
        out_specs=pl.BlockSpec((rows, wv), lambda b, h, n: (b * nb + n, h)),
        out_shape=jax.ShapeDtypeStruct((t, GLA_HEADS * dv), BF16),
        scratch_shapes=[pltpu.VMEM((heads, dv, dk), F32)],
        compiler_params=_cparams(("parallel", "parallel", "arbitrary")),
        name="gla",
    )(z, z, z, z, alr, w_alpha, b_alpha, norm_g)


def _outproj_kernel(ya_ref, yb_ref, w_ref, x_ref, gt_ref, o_ref, *, blocks_per_batch):
    ca = ya_ref.shape[1]
    y = (jnp.dot(ya_ref[...], w_ref[0:ca, :], preferred_element_type=F32)
         + jnp.dot(yb_ref[...], w_ref[ca:, :], preferred_element_type=F32))
    b = pl.program_id(0) // blocks_per_batch
    o_ref[...] = x_ref[...] + gt_ref[pl.ds(b, 1), :] * y


def _outproj(ya, yb, w_out, x, mods, layer, seq, tm=1024, tn=1024):
    t, d = x.shape
    ca, cb = ya.shape[1], yb.shape[1]
    return pl.pallas_call(
        functools.partial(_outproj_kernel, blocks_per_batch=seq // tm),
        grid=(t // tm, d // tn),
        in_specs=[
            pl.BlockSpec((tm, ca), lambda i, j: (i, 0)),
            pl.BlockSpec((tm, cb), lambda i, j: (i, 0)),
            pl.BlockSpec((ca + cb, tn), lambda i, j: (0, j)),
            pl.BlockSpec((tm, tn), lambda i, j: (i, j)),
            pl.BlockSpec((None, 8, tn), lambda i, j: (layer, 0, 2 * (d // tn) + j)),
        ],
        out_specs=pl.BlockSpec((tm, tn), lambda i, j: (i, j)),
        out_shape=jax.ShapeDtypeStruct((t, d), F32),
        compiler_params=_cparams(("parallel", "arbitrary")),
        name="out_proj",
    )(ya, yb, w_out, x, mods)


def _mlp1_kernel(h_ref, w_ref, *refs):
    n_cast = (len(refs) - 1) // 2
    o_ref = refs[n_cast]
    for src, dst in zip(refs[:n_cast], refs[n_cast + 1:]):
        dst[...] = src[...].astype(dst.dtype)
    a = jnp.maximum(jnp.dot(h_ref[...], w_ref[...], preferred_element_type=F32), 0.0)
    o_ref[...] = (a * a).astype(o_ref.dtype)


def _mlp1(h, w1, cast_items, tm=1024, tn=1024):
    t, d = h.shape
    n = w1.shape[-1]
    steps = (t // tm) * (n // tn)
    cast_in, cast_out, cast_shapes = [], [], []
    for w, lyr in cast_items:
        rows, cols = w.shape[1:]
        slab = rows // steps
        assert rows % steps == 0 and slab % BF16_ROWS == 0
        cast_in.append(pl.BlockSpec((None, slab, cols), lambda i, j, lyr=lyr: (lyr, i * (n // tn) + j, 0)))
        cast_out.append(pl.BlockSpec((slab, cols), lambda i, j: (i * (n // tn) + j, 0)))
        cast_shapes.append(jax.ShapeDtypeStruct((rows, cols), BF16))
    outs = pl.pallas_call(
        _mlp1_kernel,
        grid=(t // tm, n // tn),
        in_specs=[
            pl.BlockSpec((tm, d), lambda i, j: (i, 0)),
            pl.BlockSpec((d, tn), lambda i, j: (0, j)),
        ] + cast_in,
        out_specs=[pl.BlockSpec((tm, tn), lambda i, j: (i, j))] + cast_out,
        out_shape=[jax.ShapeDtypeStruct((t, n), BF16)] + cast_shapes,
        compiler_params=_cparams(("arbitrary", "arbitrary")),
        name="mlp_up",
    )(h, w1, *[w for w, _ in cast_items])
    return outs[0], outs[1:]


def _mlp2_kernel(a_ref, w_ref, x_ref, gt_ref, o_ref, *, blocks_per_batch):
    kk = pl.program_id(2)

    @pl.when(kk == 0)
    def _():
        o_ref[...] = jnp.zeros_like(o_ref)

    o_ref[...] += jnp.dot(a_ref[...], w_ref[...], preferred_element_type=F32)

    @pl.when(kk == pl.num_programs(2) - 1)
    def _():
        b = pl.program_id(0) // blocks_per_batch
        o_ref[...] = x_ref[...] + gt_ref[pl.ds(b, 1), :] * o_ref[...]


def _mlp2(a, w2, x, mods, layer, seq, tm=1024, tn=1024, tk=4096):
    t, d = x.shape
    kdim = a.shape[1]
    return pl.pallas_call(
        functools.partial(_mlp2_kernel, blocks_per_batch=seq // tm),
        grid=(t // tm, d // tn, kdim // tk),
        in_specs=[
            pl.BlockSpec((tm, tk), lambda i, j, k: (i, k)),
            pl.BlockSpec((tk, tn), lambda i, j, k: (k, j)),
            pl.BlockSpec((tm, tn), lambda i, j, k: (i, j)),
            pl.BlockSpec((None, 8, tn), lambda i, j, k: (layer, 0, 5 * (d // tn) + j)),
        ],
        out_specs=pl.BlockSpec((tm, tn), lambda i, j, k: (i, j)),
        out_shape=jax.ShapeDtypeStruct((t, d), F32),
        compiler_params=_cparams(("parallel", "parallel", "arbitrary")),
        name="mlp_down",
    )(a, w2, x, mods)


def _final_norm_kernel(x_ref, g_ref, o_ref):
    rows = 32
    g = g_ref[...]

    def body(r, carry):
        sl = pl.ds(pl.multiple_of(r * rows, rows), rows)
        xf = x_ref[sl, :]
        ms = jnp.mean(xf * xf, axis=-1, keepdims=True)
        o_ref[sl, :] = xf * lax.rsqrt(ms + NORM_EPS) * g
        return carry

    lax.fori_loop(0, x_ref.shape[0] // rows, body, 0)


def _final_norm(x, g, tm=512):
    t, d = x.shape
    return pl.pallas_call(
        _final_norm_kernel,
        grid=(t // tm,),
        in_specs=[pl.BlockSpec((tm, d), lambda i: (i, 0)), pl.BlockSpec((1, d), lambda i: (0, 0))],
        out_specs=pl.BlockSpec((tm, d), lambda i: (i, 0)),
        out_shape=jax.ShapeDtypeStruct((t, d), F32),
        compiler_params=_cparams(("parallel",)),
        name="final_norm",
    )(x, g)


def kernel(x, c, w_ada, b_ada, mod_table, norm1_g, w_in, conv_w, conv_b, conv_ln_g, conv_ln_b, w_alpha,
           b_alpha, gla_norm_g, w_out, norm2_g, w_mlp1, w_mlp2, final_g):
    batch, seq, d = x.shape
    depth = mod_table.shape[0]
    conv_ch = conv_w.shape[-1]
    dk = w_alpha.shape[-1] // GLA_HEADS
    dv = gla_norm_g.shape[-1]
    n_main = w_in.shape[-1] - GLA_LOWRANK
    assert batch <= 8 and mod_table.shape[1] == N_MOD and conv_w.shape[1] == CONV_WIDTH
    assert n_main == 2 * conv_ch + 2 * GLA_HEADS * (dk + dv) and w_alpha.shape[1] == GLA_LOWRANK

    w_main = w_in.astype(BF16)
    w_lr = jnp.pad(w_main[:, :, n_main:], ((0, 0), (0, 0), (0, LANES - GLA_LOWRANK)))
    wa = jnp.pad(w_alpha, ((0, 0), (0, LANES - GLA_LOWRANK), (0, 0))).astype(BF16)
    cw = jnp.pad(conv_w, ((0, 0), (0, CONV_HALO - CONV_WIDTH), (0, 0)))
    c8 = jnp.pad(c.astype(F32), ((0, 8 - batch), (0, 0)))
    stacked = (w_out, w_mlp1, w_mlp2)
    weights = None

    mods = _ada(c8, w_ada, b_ada.reshape(1, -1), mod_table.reshape(depth, 1, N_MOD * d))

    xf = x.reshape(batch * seq, d)
    for l in range(depth):
        h = _norm(xf, norm1_g.reshape(depth, 1, d), mods, l, seq, 0)
        z, alr, first_weights = _inproj(h, w_main, w_lr, n_main, l, (w_out,) if l == 0 else ())
        w_out_b, w1 = (first_weights[0], w_mlp1[0].astype(BF16)) if l == 0 else weights[:2]
        y_a = _conv(z, cw, conv_b.reshape(depth, 1, conv_ch), conv_ln_g.reshape(depth, 1, conv_ch),
                    conv_ln_b.reshape(depth, 1, conv_ch), l, seq, conv_ch)
        y_b = _gla(z, alr, wa, b_alpha.reshape(depth, 1, -1), gla_norm_g.reshape(depth, 1, dv),
                   l, batch, seq, conv_ch, dk, dv)
        xf = _outproj(y_a, y_b, w_out_b, xf, mods, l, seq)
        h = _norm(xf, norm2_g.reshape(depth, 1, d), mods, l, seq, 3)
        items = [(w_mlp2, 0)] if l == 0 else []
        items += [(w, l + 1) for w in stacked] if l + 1 < depth else []
        hid, cast = _mlp1(h, w1, items)
        w2 = cast[0] if l == 0 else weights[2]
        xf = _mlp2(hid, w2, xf, mods, l, seq)
        weights = cast[1:] if l == 0 else cast
    out = _final_norm(xf, final_g.reshape(1, d))
    return out.reshape(batch, seq, d)
```
